```python
import math
import jax, jax.numpy as jnp
from jax import lax
import numpy as np

D_MODEL = 1024
BATCH = 8
SEQ = 2048
DEPTH = 4

MEM_LEN = 256
ROPE_THETA = 10000.0
EPS = 1e-6
Q_BLOCK = 128
D_FF = 2816
MLA_HEADS = 8
MLA_NOPE = 64
MLA_ROPE = 32
MLA_V = 64
MLA_Q_LORA = 256
MLA_KV_LORA = 128
DIL_PATTERNS = ((128, 1), (512, 4), (2048, 16))
DIL_GROUPS = len(DIL_PATTERNS)
DIL_HEADS = 4
DIL_HD = 64
SPA_HEADS = 16
SPA_HD = 64
IDX_HEADS = 8
IDX_HD = 64
TOPK_MAX = 256
X_HEADS = 4
X_HD = D_MODEL // X_HEADS
FULL_ROPE_DIM = 64

EVEN_IN = MLA_Q_LORA + MLA_KV_LORA + MLA_ROPE + 3 * DIL_GROUPS * DIL_HEADS * DIL_HD
EVEN_MIX = MLA_HEADS * MLA_V + DIL_HEADS * DIL_HD
ODD_IN = SPA_HEADS * SPA_HD + 2 * SPA_HD + IDX_HEADS * IDX_HD + IDX_HD + IDX_HEADS
ODD_MIX = SPA_HEADS * SPA_HD
N_EVEN = (DEPTH + 1) // 2
N_ODD = DEPTH // 2

kernel_name = 'hybrid_mla_dilated_dsa_macaron'


def rms_norm(x, g):
    xf = x.astype(jnp.float32)
    y = xf * lax.rsqrt(jnp.mean(xf * xf, axis=-1, keepdims=True) + EPS)
    return (y * g.astype(jnp.float32)).astype(x.dtype)


def rope_tables(positions, dim):
    inv = jnp.exp(-math.log(ROPE_THETA) * jnp.arange(0, dim, 2, dtype=jnp.float32) / dim)
    ang = positions.astype(jnp.float32)[..., None] * inv
    return jnp.cos(ang), jnp.sin(ang)


def apply_rope(x, cos, sin):
    half = x.shape[-1] // 2
    xf = x.astype(jnp.float32)
    x1, x2 = xf[..., :half], xf[..., half:]
    return jnp.concatenate([x1 * cos - x2 * sin, x2 * cos + x1 * sin], axis=-1).astype(x.dtype)


def swiglu(h, w13, w2):
    gu = h @ w13
    g, u = jnp.split(gu, 2, axis=-1)
    return (jax.nn.silu(g) * u) @ w2


def causal_block_attention(q, k, v, scale):
    B, S, H, _ = q.shape
    nb = S // Q_BLOCK
    qb = q.reshape(B, nb, Q_BLOCK, H, q.shape[-1]).swapaxes(0, 1)
    kpos = jnp.arange(S)

    def blk(args):
        qi, i = args
        qpos = i * Q_BLOCK + jnp.arange(Q_BLOCK)
        s = jnp.einsum('bqhd,bkhd->bhqk', qi, k).astype(jnp.float32) * scale
        s = jnp.where(kpos[None, :] <= qpos[:, None], s, -jnp.inf)
        p = jax.nn.softmax(s, axis=-1).astype(v.dtype)
        return jnp.einsum('bhqk,bkhd->bqhd', p, v)

    out = lax.map(blk, (qb, jnp.arange(nb)))
    return out.swapaxes(0, 1).reshape(B, S, H, v.shape[-1])


def dilated_group_attention(q, k, v, dilation, w_sub):
    B, S, H, E = q.shape
    L = S // dilation
    blk = w_sub
    nb = -(-L // blk)
    Lp = nb * blk

    def to_sub(a):
        a = a.reshape(B, L, dilation, H, E).transpose(0, 2, 1, 3, 4)
        return jnp.pad(a, ((0, 0), (0, 0), (0, Lp - L), (0, 0), (0, 0)))

    def windows(a):
        a = jnp.pad(a, ((0, 0), (0, 0), (blk, 0), (0, 0), (0, 0))).reshape(B, dilation, nb + 1, blk, H, E)
        return jnp.concatenate([a[:, :, :-1], a[:, :, 1:]], axis=3)

    qb = to_sub(q).reshape(B, dilation, nb, blk, H, E)
    kw = windows(to_sub(k))
    vw = windows(to_sub(v))
    s = jnp.einsum('bdnqhe,bdnkhe->bdnhqk', qb, kw).astype(jnp.float32) * (E ** -0.5)
    qi = jnp.arange(blk)[:, None]
    ki = jnp.arange(2 * blk)[None, :]
    rel = blk + qi - ki
    band = (rel >= 0) & (rel <= w_sub)
    kstart = (jnp.arange(nb)[:, None] - 1) * blk + jnp.arange(2 * blk)[None, :]
    mask = band[None] & (kstart >= 0)[:, None, :]
    s = jnp.where(mask[None, None, :, None], s, -jnp.inf)
    lse = jax.nn.logsumexp(s, axis=-1)
    p = jnp.exp(s - lse[..., None]).astype(v.dtype)
    o = jnp.einsum('bdnhqk,bdnkhe->bdnqhe', p, vw).reshape(B, dilation, Lp, H, E)[:, :, :L]
    o = o.transpose(0, 2, 1, 3, 4).reshape(B, S, H, E)
    lse = lse.transpose(0, 1, 2, 4, 3).reshape(B, dilation, Lp, H)[:, :, :L]
    lse = lse.transpose(0, 2, 1, 3).reshape(B, S, H)
    return o, lse


def even_mixer(h, w_in, q_lora_g, kv_lora_g, w_uq, w_ukv, mla_q_g, mla_k_g, dil_q_g, dil_k_g, w_out,
               cos32, sin32, cos64, sin64):
    B, S, _ = h.shape
    cols = h @ w_in
    o1 = MLA_Q_LORA
    o2 = o1 + MLA_KV_LORA
    o3 = o2 + MLA_ROPE
    c_q = rms_norm(cols[..., :o1], q_lora_g)
    c_kv = rms_norm(cols[..., o1:o2], kv_lora_g)
    q = (c_q @ w_uq).reshape(B, S, MLA_HEADS, MLA_NOPE + MLA_ROPE)
    q_nope = rms_norm(q[..., :MLA_NOPE], mla_q_g[:MLA_NOPE])
    q_rope = apply_rope(rms_norm(q[..., MLA_NOPE:], mla_q_g[MLA_NOPE:]), cos32[:, :, None], sin32[:, :, None])
    kv = (c_kv @ w_ukv).reshape(B, S, MLA_HEADS, MLA_NOPE + MLA_V)
    k_nope = rms_norm(kv[..., :MLA_NOPE], mla_k_g[:MLA_NOPE])
    v_a = kv[..., MLA_NOPE:]
    k_rope = apply_rope(rms_norm(cols[..., o2:o3], mla_k_g[MLA_NOPE:]), cos32, sin32)
    q_a = jnp.concatenate([q_nope, q_rope], axis=-1)
    k_a = jnp.concatenate([k_nope, jnp.broadcast_to(k_rope[:, :, None], (B, S, MLA_HEADS, MLA_ROPE))], axis=-1)
    mla_out = causal_block_attention(q_a, k_a, v_a, (MLA_NOPE + MLA_ROPE) ** -0.5)
    mla_out = mla_out.reshape(B, S, MLA_HEADS * MLA_V)
    dil = cols[..., o3:].reshape(B, S, DIL_GROUPS, 3, DIL_HEADS, DIL_HD)
    dq = apply_rope(rms_norm(dil[:, :, :, 0], dil_q_g), cos64[:, :, None, None], sin64[:, :, None, None])
    dk = apply_rope(rms_norm(dil[:, :, :, 1], dil_k_g), cos64[:, :, None, None], sin64[:, :, None, None])
    dv = dil[:, :, :, 2]
    outs, lses = [], []
    for g, (window, dilation) in enumerate(DIL_PATTERNS):
        o_g, lse_g = dilated_group_attention(dq[:, :, g], dk[:, :, g], dv[:, :, g], dilation, window // dilation)
        outs.append(o_g)
        lses.append(lse_g)
    wts = jax.nn.softmax(jnp.stack(lses, axis=0), axis=0)
    dil_out = jnp.sum(wts[..., None] * jnp.stack(outs, axis=0).astype(jnp.float32), axis=0).astype(h.dtype)
    dil_out = dil_out.reshape(B, S, DIL_HEADS * DIL_HD)
    return jnp.concatenate([mla_out, dil_out], axis=-1) @ w_out


def sparse_mixer(h, w_in, q_g, k_g, w_out, cos64, sin64):
    B, S, _ = h.shape
    cols = h @ w_in
    o1 = SPA_HEADS * SPA_HD
    o2 = o1 + SPA_HD
    o3 = o2 + SPA_HD
    o4 = o3 + IDX_HEADS * IDX_HD
    o5 = o4 + IDX_HD
    q = apply_rope(rms_norm(cols[..., :o1].reshape(B, S, SPA_HEADS, SPA_HD), q_g), cos64[:, :, None], sin64[:, :, None])
    k = apply_rope(rms_norm(cols[..., o1:o2], k_g), cos64, sin64)
    v = cols[..., o2:o3]
    qi = apply_rope(cols[..., o3:o4].reshape(B, S, IDX_HEADS, IDX_HD), cos64[:, :, None], sin64[:, :, None])
    ki = apply_rope(cols[..., o4:o5], cos64, sin64)
    wi = cols[..., o5:] * (IDX_HEADS ** -0.5)
    n_keep = min(TOPK_MAX, S // 4)
    nb = S // Q_BLOCK
    kpos = jnp.arange(S)

    def to_blocks(a):
        return a.reshape((B, nb, Q_BLOCK) + a.shape[2:]).swapaxes(0, 1)

    def attend(args):
        qb, qib, wib, i = args
        qpos = i * Q_BLOCK + jnp.arange(Q_BLOCK)
        logits = jnp.einsum('bqhe,bse->bqhs', qib, ki).astype(jnp.float32) * (IDX_HD ** -0.5)
        score = jnp.einsum('bqh,bqhs->bqs', wib.astype(jnp.float32), jax.nn.relu(logits))
        score = jnp.where(kpos[None, None, :] <= qpos[None, :, None], score, -jnp.inf)
        _, idx = lax.top_k(score, n_keep)
        valid = idx <= qpos[None, :, None]
        kg = jax.vmap(lambda a, ix: a[ix])(k, idx)
        vg = jax.vmap(lambda a, ix: a[ix])(v, idx)
        s = jnp.einsum('bqhe,bqke->bqhk', qb, kg).astype(jnp.float32) * (SPA_HD ** -0.5)
        s = jnp.where(valid[:, :, None, :], s, -jnp.inf)
        p = jax.nn.softmax(s, axis=-1).astype(vg.dtype)
        return jnp.einsum('bqhk,bqke->bqhe', p, vg)

    out = lax.map(attend, (to_blocks(q), to_blocks(qi), to_blocks(wi), jnp.arange(nb)))
    out = out.swapaxes(0, 1).reshape(B, S, ODD_MIX)
    return out @ w_out


def memory_cross_attention(h, mem_n, wq, wkv, q_g, k_g, wo):
    B, S, _ = h.shape
    M = mem_n.shape[1]
    q = rms_norm((h @ wq).reshape(B, S, X_HEADS, X_HD), q_g)
    kv = (mem_n @ wkv).reshape(B, M, 2, X_HEADS, X_HD)
    k = rms_norm(kv[:, :, 0], k_g)
    v = kv[:, :, 1]
    s = jnp.einsum('bshd,bmhd->bhsm', q, k).astype(jnp.float32) * (X_HD ** -0.5)
    p = jax.nn.softmax(s, axis=-1).astype(v.dtype)
    o = jnp.einsum('bhsm,bmhd->bshd', p, v).reshape(B, S, D_MODEL)
    return o @ wo


def setup_inputs(seed: int = 0) -> dict:
    key = jax.random.key(seed)
    ks = iter(jax.random.split(key, 40))

    def w(shape, fan_in):
        return jax.random.normal(next(ks), shape, jnp.float32) * (fan_in ** -0.5)

    def gain(shape):
        return 1.0 + 0.02 * jax.random.normal(next(ks), shape, jnp.float32)

    x = jax.random.normal(next(ks), (BATCH, SEQ, D_MODEL), jnp.float32)
    mem = jax.random.normal(next(ks), (BATCH, MEM_LEN, D_MODEL), jnp.float32)
    offsets = jax.random.randint(next(ks), (BATCH, 1), 0, 4096, dtype=jnp.int32)
    positions = (offsets + jnp.arange(SEQ, dtype=jnp.int32)[None, :]).astype(jnp.int32)
    return {
        'x': x, 'mem': mem, 'positions': positions,
        'ffn1_norm': gain((DEPTH, D_MODEL)),
        'ffn1_w13': w((DEPTH, D_MODEL, 2 * D_FF), D_MODEL),
        'ffn1_w2': w((DEPTH, D_FF, D_MODEL), D_FF),
        'mix_norm': gain((DEPTH, D_MODEL)),
        'xattn_norm': gain((DEPTH, D_MODEL)),
        'mem_norm': gain((DEPTH, D_MODEL)),
        'xattn_wq': w((DEPTH, D_MODEL, D_MODEL), D_MODEL),
        'xattn_wkv': w((DEPTH, D_MODEL, 2 * D_MODEL), D_MODEL),
        'xattn_q_gain': gain((DEPTH, X_HD)),
        'xattn_k_gain': gain((DEPTH, X_HD)),
        'xattn_wo': w((DEPTH, D_MODEL, D_MODEL), D_MODEL),
        'ffn2_norm': gain((DEPTH, D_MODEL)),
        'ffn2_w13': w((DEPTH, D_MODEL, 2 * D_FF), D_MODEL),
        'ffn2_w2': w((DEPTH, D_FF, D_MODEL), D_FF),
        'even_w_in': w((N_EVEN, D_MODEL, EVEN_IN), D_MODEL),
        'mla_q_lora_norm': gain((N_EVEN, MLA_Q_LORA)),
        'mla_kv_lora_norm': gain((N_EVEN, MLA_KV_LORA)),
        'mla_w_uq': w((N_EVEN, MLA_Q_LORA, MLA_HEADS * (MLA_NOPE + MLA_ROPE)), MLA_Q_LORA),
        'mla_w_ukv': w((N_EVEN, MLA_KV_LORA, MLA_HEADS * (MLA_NOPE + MLA_V)), MLA_KV_LORA),
        'mla_q_gain': gain((N_EVEN, MLA_NOPE + MLA_ROPE)),
        'mla_k_gain': gain((N_EVEN, MLA_NOPE + MLA_ROPE)),
        'dil_q_gain': gain((N_EVEN, DIL_HD)),
        'dil_k_gain': gain((N_EVEN, DIL_HD)),
        'even_w_out': w((N_EVEN, EVEN_MIX, D_MODEL), EVEN_MIX),
        'odd_w_in': w((N_ODD, D_MODEL, ODD_IN), D_MODEL),
        'sparse_q_gain': gain((N_ODD, SPA_HD)),
        'sparse_k_gain': gain((N_ODD, SPA_HD)),
        'odd_w_out': w((N_ODD, ODD_MIX, D_MODEL), ODD_MIX),
    }


def reference(x, mem, positions, ffn1_norm, ffn1_w13, ffn1_w2, mix_norm, xattn_norm, mem_norm,
              xattn_wq, xattn_wkv, xattn_q_gain, xattn_k_gain, xattn_wo, ffn2_norm, ffn2_w13, ffn2_w2,
              even_w_in, mla_q_lora_norm, mla_kv_lora_norm, mla_w_uq, mla_w_ukv, mla_q_gain, mla_k_gain,
              dil_q_gain, dil_k_gain, even_w_out, odd_w_in, sparse_q_gain, sparse_k_gain, odd_w_out):
    cos32, sin32 = rope_tables(positions, MLA_ROPE)
    cos64, sin64 = rope_tables(positions, FULL_ROPE_DIM)
    for i in range(DEPTH):
        j = i // 2
        x = x + 0.5 * swiglu(rms_norm(x, ffn1_norm[i]), ffn1_w13[i], ffn1_w2[i])
        h = rms_norm(x, mix_norm[i])
        if i % 2 == 0:
            mix = even_mixer(h, even_w_in[j], mla_q_lora_norm[j], mla_kv_lora_norm[j], mla_w_uq[j], mla_w_ukv[j],
                             mla_q_gain[j], mla_k_gain[j], dil_q_gain[j], dil_k_gain[j], even_w_out[j],
                             cos32, sin32, cos64, sin64)
        else:
            mix = sparse_mixer(h, odd_w_in[j], sparse_q_gain[j], sparse_k_gain[j], odd_w_out[j], cos64, sin64)
        x = x + mix
        x = x + memory_cross_attention(rms_norm(x, xattn_norm[i]), rms_norm(mem, mem_norm[i]), xattn_wq[i],
                                       xattn_wkv[i], xattn_q_gain[i], xattn_k_gain[i], xattn_wo[i])
        x = x + 0.5 * swiglu(rms_norm(x, ffn2_norm[i]), ffn2_w13[i], ffn2_w2[i])
    return x
```

```python
import functools
import math

import jax
import jax.numpy as jnp
import numpy as np
from jax import lax
from jax.experimental import pallas as pl
from jax.experimental.pallas import tpu as pltpu

D_MODEL = 1024
DEPTH = 4
EPS = 1e-6
D_FF = 2816
ROPE_THETA = 10000.0
MLA_HEADS, MLA_NOPE, MLA_ROPE, MLA_V = 8, 64, 32, 64
MLA_Q_LORA, MLA_KV_LORA = 256, 128
DIL_PATTERNS = ((128, 1), (512, 4), (2048, 16))
DIL_HEADS, DIL_HD = 4, 64
SPA_HEADS, SPA_HD = 16, 64
IDX_HEADS, IDX_HD = 8, 64
TOPK_MAX = 256
X_HEADS = 4
X_HD = D_MODEL // X_HEADS

LANES = 128
MXU_WIDTH = 256
VMEM_LIMIT_BYTES = 56 * 1024 * 1024

_MX = jnp.bfloat16
_F32 = jnp.float32
_NEG = -1e30


def _dot(a, b):
    return jnp.dot(a, b, preferred_element_type=_F32)


def _dot_nt(a, b):
    return lax.dot_general(a, b, (((1,), (1,)), ((), ())), preferred_element_type=_F32)


def _params(sem):
    return pltpu.CompilerParams(dimension_semantics=sem, vmem_limit_bytes=VMEM_LIMIT_BYTES)


def _row_rmsnorm(x, g):
    return x * lax.rsqrt(jnp.mean(x * x, axis=-1, keepdims=True) + EPS) * g


def _seg_sumsq(y, seg_ref):
    y2 = y * y
    hi = y2.astype(_MX)
    lo = (y2 - hi.astype(_F32)).astype(_MX)
    s = seg_ref[...]
    return _dot(hi, s) + _dot(lo, s)


def _swap_lanes(z, d):
    lane = lax.broadcasted_iota(jnp.int32, z.shape, 1)
    fwd = pltpu.roll(z, LANES - d, 1)
    bwd = pltpu.roll(z, d, 1)
    return jnp.where((lane & d) == 0, fwd, bwd)


def _rope_tile(z, d, cos, sin_signed):
    return z * cos + _swap_lanes(z, d) * sin_signed


FFN_TM = 512
FFN_TF = 1408
FFN_NK = D_FF // FFN_TF


def _ffn_kernel(x_ref, g_ref, wg_ref, wu_ref, w2_ref, o_ref, h_scr, acc_scr):
    k = pl.program_id(1)

    @pl.when(k == 0)
    def _():
        h_scr[...] = _row_rmsnorm(x_ref[...], g_ref[...]).astype(_MX)
        acc_scr[...] = jnp.zeros_like(acc_scr)

    h = h_scr[...]
    gate = _dot(h, wg_ref[...])
    up = _dot(h, wu_ref[...])
    act = (jax.nn.silu(gate) * up).astype(_MX)
    acc_scr[...] += _dot(act, w2_ref[...])

    @pl.when(k == FFN_NK - 1)
    def _():
        o_ref[...] = x_ref[...] + 0.5 * acc_scr[...]


def _ffn(x, g, w13, w2, layer):
    n = x.shape[0]
    return pl.pallas_call(
        _ffn_kernel,
        grid=(n // FFN_TM, FFN_NK),
        in_specs=[
            pl.BlockSpec((FFN_TM, D_MODEL), lambda i, k: (i, 0)),
            pl.BlockSpec((None, 1, D_MODEL), lambda i, k: (layer, 0, 0)),
            pl.BlockSpec((None, D_MODEL, FFN_TF), lambda i, k: (layer, 0, k)),
            pl.BlockSpec((None, D_MODEL, FFN_TF), lambda i, k: (layer, 0, FFN_NK + k)),
            pl.BlockSpec((None, FFN_TF, D_MODEL), lambda i, k: (layer, k, 0)),
        ],
        out_specs=pl.BlockSpec((FFN_TM, D_MODEL), lambda i, k: (i, 0)),
        out_shape=jax.ShapeDtypeStruct(x.shape, _F32),
        scratch_shapes=[pltpu.VMEM((FFN_TM, D_MODEL), _MX), pltpu.VMEM((FFN_TM, D_MODEL), _F32)],
        compiler_params=_params(("parallel", "arbitrary")),
        name="ffn",
    )(x, g, w13, w13, w2)


PROJ_TM = 256
EVEN_IN_PAD = 2816
DIL_COLS = 2304


def _even_proj_kernel(x_ref, g_ref, win_ref, wq_ref, wk_ref, wv_ref, pin_ref, pq_ref, pk_ref,
                      s64_ref, s128_ref, s256_ref, c64_ref, n64_ref, cm_ref, nm_ref,
                      qm_ref, km_ref, vm_ref, dil_ref):
    h = _row_rmsnorm(x_ref[...], g_ref[...]).astype(_MX)
    c64, n64, cm, nm = c64_ref[...], n64_ref[...], cm_ref[...], nm_ref[...]

    def normed(y, seg_ref, p_ref, c0):
        ss = _seg_sumsq(y, seg_ref)
        return y * lax.rsqrt(ss * p_ref[0:1, c0:c0 + MXU_WIDTH] + EPS) * p_ref[1:2, c0:c0 + MXU_WIDTH]

    cq = normed(_dot(h, win_ref[:, 0:256]), s256_ref, pin_ref, 0).astype(_MX)
    z = normed(_dot(h, win_ref[:, 256:512]), s128_ref, pin_ref, 256)
    ckv = z[:, :LANES].astype(_MX)
    k_rope = _rope_tile(z[:, LANES:], MLA_ROPE // 2, cm, nm)

    for j in range(DIL_COLS // MXU_WIDTH):
        c0 = 512 + MXU_WIDTH * j
        y = _dot(h, win_ref[:, c0:c0 + MXU_WIDTH])
        if j % 3 < 2:
            z = normed(y, s64_ref, pin_ref, c0)
            for t in range(2):
                tile = _rope_tile(z[:, LANES * t:LANES * (t + 1)], DIL_HD // 2, c64, n64)
                dil_ref[:, MXU_WIDTH * j + LANES * t:MXU_WIDTH * j + LANES * (t + 1)] = tile.astype(_MX)
        else:
            dil_ref[:, MXU_WIDTH * j:MXU_WIDTH * (j + 1)] = y.astype(_MX)

    for j in range(MLA_HEADS * LANES // MXU_WIDTH):
        c0 = MXU_WIDTH * j
        z = normed(_dot(cq, wq_ref[:, c0:c0 + MXU_WIDTH]), s64_ref, pq_ref, c0)
        for t in range(2):
            tile = _rope_tile(z[:, LANES * t:LANES * (t + 1)], MLA_ROPE // 2, cm, nm)
            qm_ref[:, c0 + LANES * t:c0 + LANES * (t + 1)] = tile.astype(_MX)
    for j in range(MLA_HEADS * LANES // MXU_WIDTH):
        c0 = MXU_WIDTH * j
        z = normed(_dot(ckv, wk_ref[:, c0:c0 + MXU_WIDTH]), s64_ref, pk_ref, c0)
        for t in range(2):
            tile = z[:, LANES * t:LANES * (t + 1)] + k_rope
            km_ref[:, c0 + LANES * t:c0 + LANES * (t + 1)] = tile.astype(_MX)
    for j in range(MLA_HEADS * MLA_V // MXU_WIDTH):
        c0 = MXU_WIDTH * j
        vm_ref[:, c0:c0 + MXU_WIDTH] = _dot(ckv, wv_ref[:, c0:c0 + MXU_WIDTH]).astype(_MX)


def _const_spec(shape):
    nd = len(shape)
    return pl.BlockSpec(shape, lambda i: (0,) * nd)


def _even_proj(x, g, win, wq, wk, wv, pin, pq, pk, segs, tabs):
    n = x.shape[0]
    tm = PROJ_TM
    row = lambda w: pl.BlockSpec((tm, w), lambda i: (i, 0))
    s64, s128, s256 = segs
    c64, n64, cm, nm = tabs
    return pl.pallas_call(
        _even_proj_kernel,
        grid=(n // tm,),
        in_specs=[row(D_MODEL), _const_spec(g.shape), _const_spec(win.shape), _const_spec(wq.shape),
                  _const_spec(wk.shape), _const_spec(wv.shape), _const_spec(pin.shape),
                  _const_spec(pq.shape), _const_spec(pk.shape), _const_spec(s64.shape),
                  _const_spec(s128.shape), _const_spec(s256.shape),
                  row(LANES), row(LANES), row(LANES), row(LANES)],
        out_specs=[row(MLA_HEADS * LANES), row(MLA_HEADS * LANES), row(MLA_HEADS * MLA_V), row(DIL_COLS)],
        out_shape=[jax.ShapeDtypeStruct((n, MLA_HEADS * LANES), _MX),
                   jax.ShapeDtypeStruct((n, MLA_HEADS * LANES), _MX),
                   jax.ShapeDtypeStruct((n, MLA_HEADS * MLA_V), _MX),
                   jax.ShapeDtypeStruct((n, DIL_COLS), _MX)],
        compiler_params=_params(("parallel",)),
        name="even_proj",
    )(x, g, win, wq, wk, wv, pin, pq, pk, s64, s128, s256, c64, n64, cm, nm)


MLA_TQ = 256
MLA_TK = 256


def _mla_attn_kernel(q_ref, k_ref, v_ref, o_ref, m_scr, l_scr, acc_scr):
    qi = pl.program_id(2)
    scale = (MLA_NOPE + MLA_ROPE) ** -0.5
    m_scr[...] = jnp.full_like(m_scr, _NEG)
    l_scr[...] = jnp.zeros_like(l_scr)
    acc_scr[...] = jnp.zeros_like(acc_scr)
    row = lax.broadcasted_iota(jnp.int32, (MLA_TQ, MLA_TK), 0)
    col = lax.broadcasted_iota(jnp.int32, (MLA_TQ, MLA_TK), 1)

    def chunk(c, carry):
        k0 = pl.multiple_of(c * MLA_TK, MLA_TK)
        visible = (k0 + col) <= (qi * MLA_TQ + row)
        v2 = v_ref[pl.ds(k0, MLA_TK), :]
        for hh in range(2):
            q = q_ref[:, LANES * hh:LANES * (hh + 1)]
            k = k_ref[pl.ds(k0, MLA_TK), LANES * hh:LANES * (hh + 1)]
            s = jnp.where(visible, _dot_nt(q, k) * scale, _NEG)
            m_old = m_scr[hh]
            m_new = jnp.maximum(m_old, jnp.max(s, axis=-1, keepdims=True))
            alpha = jnp.exp(m_old - m_new)
            p = jnp.exp(s - m_new)
            l_scr[hh] = alpha * l_scr[hh] + jnp.sum(p, axis=-1, keepdims=True)
            acc_scr[hh] = alpha * acc_scr[hh] + _dot(p.astype(_MX), v2)
            m_scr[hh] = m_new
        return carry

    lax.fori_loop(0, qi + 1, chunk, 0)
    lane = lax.broadcasted_iota(jnp.int32, (MLA_TQ, LANES), 1)
    o0 = acc_scr[0] / l_scr[0]
    o1 = acc_scr[1] / l_scr[1]
    o_ref[...] = jnp.where(lane < MLA_V, o0, o1).astype(o_ref.dtype)


def _mla_attn(qm, km, vm, batch, seq):
    n = qm.shape[0]
    nq = seq // MLA_TQ
    pairs = MLA_HEADS // 2
    return pl.pallas_call(
        _mla_attn_kernel,
        grid=(batch, pairs, nq),
        in_specs=[
            pl.BlockSpec((MLA_TQ, 2 * LANES), lambda b, p, i: (b * nq + i, p)),
            pl.BlockSpec((seq, 2 * LANES), lambda b, p, i: (b, p)),
            pl.BlockSpec((seq, LANES), lambda b, p, i: (b, p)),
        ],
        out_specs=pl.BlockSpec((MLA_TQ, LANES), lambda b, p, i: (b * nq + i, p)),
        out_shape=jax.ShapeDtypeStruct((n, MLA_HEADS * MLA_V), _MX),
        scratch_shapes=[pltpu.VMEM((2, MLA_TQ, 1), _F32), pltpu.VMEM((2, MLA_TQ, 1), _F32),
                        pltpu.VMEM((2, MLA_TQ, LANES), _F32)],
        compiler_params=_params(("parallel", "parallel", "arbitrary")),
        name="mla_attn",
    )(qm, km, vm)


DIL_BLK = 128


def _dil_attn_kernel(q_ref, k_ref, v_ref, o_ref, lse_ref, *, nb):
    scale = DIL_HD ** -0.5
    qi = lax.broadcasted_iota(jnp.int32, (DIL_BLK, DIL_BLK), 0)
    ki = lax.broadcasted_iota(jnp.int32, (DIL_BLK, DIL_BLK), 1)
    lane = lax.broadcasted_iota(jnp.int32, (DIL_BLK, LANES), 1)
    cur_ok = ki <= qi

    def block(n, carry):
        r0 = pl.multiple_of(n * DIL_BLK, DIL_BLK)
        p0 = pl.multiple_of(jnp.maximum(n - 1, 0) * DIL_BLK, DIL_BLK)
        prev_ok = ki >= qi + jnp.where(n > 0, 0, DIL_BLK)
        for pr in range(DIL_HEADS // 2):
            cols = slice(LANES * pr, LANES * (pr + 1))
            q2 = q_ref[pl.ds(r0, DIL_BLK), cols]
            kc = k_ref[pl.ds(r0, DIL_BLK), cols]
            kp = k_ref[pl.ds(p0, DIL_BLK), cols]
            vc = v_ref[pl.ds(r0, DIL_BLK), cols]
            vp = v_ref[pl.ds(p0, DIL_BLK), cols]
            outs, lses = [], []
            for half in range(2):
                mine = (lane < DIL_HD) if half == 0 else (lane >= DIL_HD)
                qh = jnp.where(mine, q2, jnp.zeros_like(q2))
                sc = jnp.where(cur_ok, _dot_nt(qh, kc) * scale, _NEG)
                sp = jnp.where(prev_ok, _dot_nt(qh, kp) * scale, _NEG)
                m = jnp.maximum(jnp.max(sc, axis=-1, keepdims=True), jnp.max(sp, axis=-1, keepdims=True))
                pc = jnp.exp(sc - m)
                pp = jnp.exp(sp - m)
                l = jnp.sum(pc, axis=-1, keepdims=True) + jnp.sum(pp, axis=-1, keepdims=True)
                o = (_dot(pc.astype(_MX), vc) + _dot(pp.astype(_MX), vp)) / l
                outs.append(o)
                lses.append(m + jnp.log(l))
            o_ref[pl.ds(r0, DIL_BLK), cols] = jnp.where(lane < DIL_HD, outs[0], outs[1])
            lse_ref[pl.ds(r0, DIL_BLK), cols] = jnp.where(lane < DIL_HD, lses[0], lses[1])
        return carry

    lax.fori_loop(0, nb, block, 0)


def _dil_attn(dil, group, dilation, batch, seq):
    n = dil.shape[0]
    length = seq // dilation
    nb = length // DIL_BLK
    width = DIL_HEADS * DIL_HD
    per_res = DIL_COLS // width
    view = dil.reshape(n // dilation, dilation * DIL_COLS)
    spec = lambda part: pl.BlockSpec((length, width), lambda b, r: (b, r * per_res + group * 3 + part))
    ospec = pl.BlockSpec((length, width), lambda b, r: (b, r))
    o, lse = pl.pallas_call(
        functools.partial(_dil_attn_kernel, nb=nb),
        grid=(batch, dilation),
        in_specs=[spec(0), spec(1), spec(2)],
        out_specs=[ospec, ospec],
        out_shape=[jax.ShapeDtypeStruct((n // dilation, dilation * width), _F32)] * 2,
        compiler_params=_params(("parallel", "parallel")),
        name=f"dil_attn_d{dilation}",
    )(view, view, view)
    return o.reshape(n, width), lse.reshape(n, width)


OUT_TM = 512


def _even_out_kernel(x_ref, mla_ref, o0_ref, o1_ref, o2_ref, l0_ref, l1_ref, l2_ref, w_ref, out_ref):
    l0, l1, l2 = l0_ref[...], l1_ref[...], l2_ref[...]
    m = jnp.maximum(jnp.maximum(l0, l1), l2)
    e0, e1, e2 = jnp.exp(l0 - m), jnp.exp(l1 - m), jnp.exp(l2 - m)
    dil = (e0 * o0_ref[...] + e1 * o1_ref[...] + e2 * o2_ref[...]) / (e0 + e1 + e2)
    nm = MLA_HEADS * MLA_V
    mix = _dot(mla_ref[...], w_ref[0:nm, :]) + _dot(dil.astype(_MX), w_ref[nm:, :])
    out_ref[...] = x_ref[...] + mix


def _even_out(x, mla, outs, lses, w):
    n = x.shape[0]
    tm = OUT_TM
    row = lambda wd: pl.BlockSpec((tm, wd), lambda i: (i, 0))
    wd = DIL_HEADS * DIL_HD
    return pl.pallas_call(
        _even_out_kernel,
        grid=(n // tm,),
        in_specs=[row(D_MODEL), row(MLA_HEADS * MLA_V)] + [row(wd)] * 6 + [_const_spec(w.shape)],
        out_specs=row(D_MODEL),
        out_shape=jax.ShapeDtypeStruct(x.shape, _F32),
        compiler_params=_params(("parallel",)),
        name="even_out",
    )(x, mla, *outs, *lses, w)


ODD_IN_PAD = 2048


def _odd_proj_kernel(x_ref, g_ref, win_ref, pin_ref, s64_ref, c64_ref, n64_ref,
                     q_ref, kv_ref, qi_ref, ki_ref, wit_ref):
    h = _row_rmsnorm(x_ref[...], g_ref[...]).astype(_MX)
    c64, n64 = c64_ref[...], n64_ref[...]

    def normed(y, c0):
        ss = _seg_sumsq(y, s64_ref)
        return y * lax.rsqrt(ss * pin_ref[0:1, c0:c0 + MXU_WIDTH] + EPS) * pin_ref[1:2, c0:c0 + MXU_WIDTH]

    def rope64(t):
        return _rope_tile(t, SPA_HD // 2, c64, n64)

    nq = SPA_HEADS * SPA_HD
    for j in range(nq // MXU_WIDTH):
        c0 = MXU_WIDTH * j
        z = normed(_dot(h, win_ref[:, c0:c0 + MXU_WIDTH]), c0)
        for t in range(2):
            q_ref[:, c0 + LANES * t:c0 + LANES * (t + 1)] = rope64(z[:, LANES * t:LANES * (t + 1)]).astype(_MX)
    y = _dot(h, win_ref[:, nq:nq + MXU_WIDTH])
    z = normed(y, nq)
    kv_ref[:, 0:LANES] = rope64(z[:, :LANES]).astype(_MX)
    kv_ref[:, LANES:] = y[:, LANES:].astype(_MX)
    c_qi = nq + MXU_WIDTH
    for j in range(IDX_HEADS * IDX_HD // MXU_WIDTH):
        c0 = c_qi + MXU_WIDTH * j
        y = _dot(h, win_ref[:, c0:c0 + MXU_WIDTH])
        for t in range(2):
            qi_ref[:, MXU_WIDTH * j + LANES * t:MXU_WIDTH * j + LANES * (t + 1)] = (
                rope64(y[:, LANES * t:LANES * (t + 1)]).astype(_MX))
    c0 = c_qi + IDX_HEADS * IDX_HD
    y = _dot(h, win_ref[:, c0:c0 + MXU_WIDTH])
    ki_ref[...] = rope64(y[:, :LANES]).astype(_MX)
    wi = (y[:, LANES:] * (IDX_HEADS ** -0.5)) * (IDX_HD ** -0.5)
    wit_ref[...] = wi.T[0:IDX_HEADS, :]


def _odd_proj(x, g, win, pin, s64, tabs):
    n = x.shape[0]
    tm = PROJ_TM
    row = lambda w: pl.BlockSpec((tm, w), lambda i: (i, 0))
    c64, n64 = tabs
    return pl.pallas_call(
        _odd_proj_kernel,
        grid=(n // tm,),
        in_specs=[row(D_MODEL), _const_spec(g.shape), _const_spec(win.shape), _const_spec(pin.shape),
                  _const_spec(s64.shape), row(LANES), row(LANES)],
        out_specs=[row(SPA_HEADS * SPA_HD), row(2 * LANES), row(IDX_HEADS * IDX_HD), row(LANES),
                   pl.BlockSpec((IDX_HEADS, tm), lambda i: (0, i))],
        out_shape=[jax.ShapeDtypeStruct((n, SPA_HEADS * SPA_HD), _MX),
                   jax.ShapeDtypeStruct((n, 2 * LANES), _MX),
                   jax.ShapeDtypeStruct((n, IDX_HEADS * IDX_HD), _MX),
                   jax.ShapeDtypeStruct((n, LANES), _MX),
                   jax.ShapeDtypeStruct((IDX_HEADS, n), _F32)],
        compiler_params=_params(("parallel",)),
        name="odd_proj",
    )(x, g, win, pin, s64, c64, n64)


SPA_TQ = 128
SPA_TK = 256
_INT_MIN = -2 ** 31


def _sparse_attn_kernel(q_ref, kv_ref, qi_ref, ki_ref, wit_ref, o_ref, key_scr, m_scr, acc_scr, *, n_keep):
    qb = pl.program_id(1)
    n_chunks = qb // (SPA_TK // SPA_TQ) + 1
    lane = lax.broadcasted_iota(jnp.int32, (SPA_TQ, LANES), 1)
    left = lane < SPA_HD

    def stack_heads(ref, n_pairs):
        parts = []
        for pr in range(n_pairs):
            t = ref[:, LANES * pr:LANES * (pr + 1)]
            parts.append(jnp.where(left, t, jnp.zeros_like(t)))
            parts.append(jnp.where(left, jnp.zeros_like(t), t))
        return jnp.concatenate(parts, axis=0)

    qi_stack = stack_heads(qi_ref, IDX_HEADS // 2)
    wit = wit_ref[...]
    krow = lax.broadcasted_iota(jnp.int32, (SPA_TK, SPA_TQ), 0)
    qpos = qb * SPA_TQ + lax.broadcasted_iota(jnp.int32, (SPA_TK, SPA_TQ), 1)

    def score_chunk(c, carry):
        k0 = pl.multiple_of(c * SPA_TK, SPA_TK)
        logits = _dot_nt(ki_ref[pl.ds(k0, SPA_TK), :], qi_stack)
        sc = jnp.zeros((SPA_TK, SPA_TQ), _F32)
        for hd in range(IDX_HEADS):
            sc = sc + wit[hd:hd + 1, :] * jnp.maximum(logits[:, SPA_TQ * hd:SPA_TQ * (hd + 1)], 0.0)
        sc = jnp.where((k0 + krow) <= qpos, sc, -jnp.inf)
        bits = pltpu.bitcast(sc, jnp.int32)
        key_scr[c] = bits ^ ((bits >> 31) & jnp.int32(0x7FFFFFFF))
        return carry

    lax.fori_loop(0, n_chunks, score_chunk, 0)

    def count(pred):
        def body(c, acc):
            hit = pred(key_scr[c], c * SPA_TK)
            return acc + jnp.sum(hit.astype(jnp.int32).reshape(SPA_TK // 8, 8, SPA_TQ), axis=0)
        acc = lax.fori_loop(0, n_chunks, body, jnp.zeros((8, SPA_TQ), jnp.int32))
        return jnp.sum(acc, axis=0, keepdims=True)

    def refine(t, cand):
        return jnp.where(count(lambda k, k0: k >= cand) >= n_keep, cand, t)

    t = jnp.full((1, SPA_TQ), _INT_MIN, jnp.int32)
    t = refine(t, jnp.zeros((1, SPA_TQ), jnp.int32))
    t = lax.fori_loop(0, 31, lambda i, t: refine(t, t | (jnp.int32(1) << (30 - i))), t)
    need = n_keep - count(lambda k, k0: k > t)

    def refine_pos(j, cand):
        below = count(lambda k, k0: (k == t) & ((k0 + krow) < cand))
        return jnp.where(below < need, cand, j)

    n_bits = (key_scr.shape[0] * SPA_TK).bit_length()
    last = lax.fori_loop(0, n_bits, lambda i, j: refine_pos(j, j | (jnp.int32(1) << (n_bits - 1 - i))),
                         jnp.zeros((1, SPA_TQ), jnp.int32))

    scale = SPA_HD ** -0.5
    q_stack = stack_heads(q_ref, SPA_HEADS // 2)
    m_scr[...] = jnp.full_like(m_scr, _NEG)
    acc_scr[...] = jnp.zeros_like(acc_scr)
    vlane = lax.broadcasted_iota(jnp.int32, (SPA_TK, LANES), 1)

    def attend_chunk(c, carry):
        k0 = pl.multiple_of(c * SPA_TK, SPA_TK)
        keys = key_scr[c]
        kpos = k0 + krow
        chosen = ((keys > t) | ((keys == t) & (kpos <= last))) & (kpos <= qpos)
        bias = jnp.where(chosen, 0.0, _NEG).T
        kk = kv_ref[pl.ds(k0, SPA_TK), 0:LANES]
        vv = kv_ref[pl.ds(k0, SPA_TK), LANES:]
        v1 = jnp.where(vlane < SPA_HD, vv, jnp.ones_like(vv))
        s = _dot_nt(q_stack, kk) * scale
        s = (s.reshape(SPA_HEADS, SPA_TQ, SPA_TK) + bias[None]).reshape(SPA_HEADS * SPA_TQ, SPA_TK)
        m_old = m_scr[...]
        m_new = jnp.maximum(m_old, jnp.max(s, axis=-1, keepdims=True))
        p = jnp.exp(s - m_new)
        acc_scr[...] = jnp.exp(m_old - m_new) * acc_scr[...] + _dot(p.astype(_MX), v1)
        m_scr[...] = m_new
        return carry

    lax.fori_loop(0, n_chunks, attend_chunk, 0)
    for pr in range(SPA_HEADS // 2):
        a = acc_scr[SPA_TQ * (2 * pr):SPA_TQ * (2 * pr + 1), :]
        b = acc_scr[SPA_TQ * (2 * pr + 1):SPA_TQ * (2 * pr + 2), :]
        b_sw = pltpu.roll(b, SPA_HD, 1)
        tile = jnp.where(left, a / pltpu.roll(a, SPA_HD, 1), b_sw / b)
        o_ref[:, LANES * pr:LANES * (pr + 1)] = tile.astype(o_ref.dtype)


def _sparse_attn(q, kv, qi, ki, wit, batch, seq):
    n = q.shape[0]
    nq = seq // SPA_TQ
    n_keep = min(TOPK_MAX, seq // 4)
    return pl.pallas_call(
        functools.partial(_sparse_attn_kernel, n_keep=n_keep),
        grid=(batch, nq),
        in_specs=[
            pl.BlockSpec((SPA_TQ, SPA_HEADS * SPA_HD), lambda b, i: (b * nq + i, 0)),
            pl.BlockSpec((seq, 2 * LANES), lambda b, i: (b, 0)),
            pl.BlockSpec((SPA_TQ, IDX_HEADS * IDX_HD), lambda b, i: (b * nq + i, 0)),
            pl.BlockSpec((seq, LANES), lambda b, i: (b, 0)),
            pl.BlockSpec((IDX_HEADS, SPA_TQ), lambda b, i: (0, b * nq + i)),
        ],
        out_specs=pl.BlockSpec((SPA_TQ, SPA_HEADS * SPA_HD), lambda b, i: (b * nq + i, 0)),
        out_shape=jax.ShapeDtypeStruct((n, SPA_HEADS * SPA_HD), _MX),
        scratch_shapes=[pltpu.VMEM((seq // SPA_TK, SPA_TK, SPA_TQ), jnp.int32),
                        pltpu.VMEM((SPA_HEADS * SPA_TQ, 1), _F32),
                        pltpu.VMEM((SPA_HEADS * SPA_TQ, LANES), _F32)],
        compiler_params=_params(("parallel", "arbitrary")),
        name="sparse_attn",
    )(q, kv, qi, ki, wit)


def _proj_residual_kernel(x_ref, a_ref, w_ref, out_ref):
    out_ref[...] = x_ref[...] + _dot(a_ref[...], w_ref[...])


def _proj_residual(x, a, w):
    n = x.shape[0]
    tm = OUT_TM
    return pl.pallas_call(
        _proj_residual_kernel,
        grid=(n // tm,),
        in_specs=[pl.BlockSpec((tm, D_MODEL), lambda i: (i, 0)),
                  pl.BlockSpec((tm, a.shape[1]), lambda i: (i, 0)), _const_spec(w.shape)],
        out_specs=pl.BlockSpec((tm, D_MODEL), lambda i: (i, 0)),
        out_shape=jax.ShapeDtypeStruct(x.shape, _F32),
        compiler_params=_params(("parallel",)),
        name="odd_out",
    )(x, a, w)


def _mem_kv_kernel(mem_ref, g_ref, w_ref, kg_ref, k_ref, v_ref):
    h = _row_rmsnorm(mem_ref[...], g_ref[...]).astype(_MX)
    kg = kg_ref[...]
    for hd in range(X_HEADS):
        cols = slice(X_HD * hd, X_HD * (hd + 1))
        y = _dot(h, w_ref[:, cols])
        k_ref[:, cols] = _row_rmsnorm(y, kg).astype(_MX)
    for hd in range(X_HEADS):
        cols = slice(X_HD * hd, X_HD * (hd + 1))
        v_ref[:, cols] = _dot(h, w_ref[:, D_MODEL + X_HD * hd:D_MODEL + X_HD * (hd + 1)]).astype(_MX)


def _mem_kv(mem, g, wkv, kg):
    n, m = mem.shape[0], 256
    row = pl.BlockSpec((m, D_MODEL), lambda i: (i, 0))
    return pl.pallas_call(
        _mem_kv_kernel,
        grid=(n // m,),
        in_specs=[row, _const_spec(g.shape), _const_spec(wkv.shape), _const_spec(kg.shape)],
        out_specs=[row, row],
        out_shape=[jax.ShapeDtypeStruct((n, D_MODEL), _MX)] * 2,
        compiler_params=_params(("parallel",)),
        name="mem_kv",
    )(mem, g, wkv, kg)


XATT_TM = 512


def _xattn_kernel(x_ref, g_ref, wq_ref, qg_ref, k_ref, v_ref, wo_ref, out_ref, o_scr):
    x = x_ref[...]
    h = _row_rmsnorm(x, g_ref[...]).astype(_MX)
    qg = qg_ref[...]
    scale = X_HD ** -0.5
    for hd in range(X_HEADS):
        cols = slice(X_HD * hd, X_HD * (hd + 1))
        q = _row_rmsnorm(_dot(h, wq_ref[:, cols]), qg).astype(_MX)
        s = _dot_nt(q, k_ref[:, cols]) * scale
        p = jnp.exp(s - jnp.max(s, axis=-1, keepdims=True))
        p = p / jnp.sum(p, axis=-1, keepdims=True)
        o_scr[:, cols] = _dot(p.astype(_MX), v_ref[:, cols]).astype(_MX)
    out_ref[...] = x + _dot(o_scr[...], wo_ref[...])


def _xattn(x, g, wq, qg, k, v, wo, batch, seq, mem_len):
    n = x.shape[0]
    tm = XATT_TM
    nt = seq // tm
    row = pl.BlockSpec((tm, D_MODEL), lambda b, i: (b * nt + i, 0))
    const = lambda a: pl.BlockSpec(a.shape, lambda b, i: (0,) * a.ndim)
    memspec = pl.BlockSpec((mem_len, D_MODEL), lambda b, i: (b, 0))
    return pl.pallas_call(
        _xattn_kernel,
        grid=(batch, nt),
        in_specs=[row, const(g), const(wq), const(qg), memspec, memspec, const(wo)],
        out_specs=row,
        out_shape=jax.ShapeDtypeStruct(x.shape, _F32),
        scratch_shapes=[pltpu.VMEM((tm, D_MODEL), _MX)],
        compiler_params=_params(("parallel", "parallel")),
        name="xattn",
    )(x, g, wq, qg, k, v, wo)


def _seg_matrix(seg):
    idx = np.arange(MXU_WIDTH) // seg
    return jnp.asarray(idx[:, None] == idx[None, :], _MX)


def _rope_tables(positions, dim):
    inv = jnp.exp(-math.log(ROPE_THETA) * jnp.arange(0, dim, 2, dtype=_F32) / dim)
    ang = positions.astype(_F32).reshape(-1)[:, None] * inv
    return jnp.cos(ang), jnp.sin(ang)


def _tables(positions):
    cos64, sin64 = _rope_tables(positions, DIL_HD)
    cos32, sin32 = _rope_tables(positions, MLA_ROPE)
    n = cos64.shape[0]
    c64 = jnp.tile(cos64, (1, 4))
    n64 = jnp.tile(jnp.concatenate([-sin64, sin64], axis=1), (1, 2))
    ones, zeros = jnp.ones((n, MLA_NOPE), _F32), jnp.zeros((n, MLA_NOPE), _F32)
    pad1, pad0 = jnp.ones((n, 32), _F32), jnp.zeros((n, 32), _F32)
    cm = jnp.concatenate([ones, cos32, cos32, pad1], axis=1)
    nm = jnp.concatenate([zeros, -sin32, sin32, pad0], axis=1)
    return c64, n64, cm, nm


def _even_weights(w_in, q_lora_g, kv_lora_g, w_uq, w_ukv, q_g, k_g, dq_g, dk_g, w_out):
    o1, o2, o3 = MLA_Q_LORA, MLA_Q_LORA + MLA_KV_LORA, MLA_Q_LORA + MLA_KV_LORA + MLA_ROPE
    d = w_in.shape[0]
    z = lambda w: jnp.zeros((d, w), _F32)
    win = jnp.concatenate([w_in[:, :o2], z(64), w_in[:, o2:o3], z(32), w_in[:, o3:]], axis=1).astype(_MX)
    ones = lambda w: jnp.ones((w,), _F32)
    zeros = lambda w: jnp.zeros((w,), _F32)
    grp_gain = jnp.concatenate([jnp.tile(dq_g, DIL_HEADS), jnp.tile(dk_g, DIL_HEADS), ones(256)])
    pin = jnp.stack([
        jnp.concatenate([ones(256) / 256, ones(128) / 128, ones(128) / MLA_ROPE, ones(DIL_COLS) / DIL_HD]),
        jnp.concatenate([q_lora_g, kv_lora_g, zeros(64), k_g[MLA_NOPE:], zeros(32), jnp.tile(grp_gain, 3)]),
    ])
    wq = jnp.pad(w_uq.reshape(MLA_Q_LORA, MLA_HEADS, MLA_NOPE + MLA_ROPE), ((0, 0), (0, 0), (0, 32)))
    wq = wq.reshape(MLA_Q_LORA, MLA_HEADS * LANES).astype(_MX)
    pq = jnp.stack([
        jnp.tile(jnp.concatenate([ones(64) / MLA_NOPE, ones(64) / MLA_ROPE]), MLA_HEADS),
        jnp.tile(jnp.concatenate([q_g, zeros(32)]), MLA_HEADS),
    ])
    ukv = w_ukv.reshape(MLA_KV_LORA, MLA_HEADS, MLA_NOPE + MLA_V)
    wk = jnp.pad(ukv[:, :, :MLA_NOPE], ((0, 0), (0, 0), (0, 64))).reshape(MLA_KV_LORA, MLA_HEADS * LANES).astype(_MX)
    wv = ukv[:, :, MLA_NOPE:].reshape(MLA_KV_LORA, MLA_HEADS * MLA_V).astype(_MX)
    pk = jnp.stack([
        jnp.tile(ones(LANES) / MLA_NOPE, MLA_HEADS),
        jnp.tile(jnp.concatenate([k_g[:MLA_NOPE], zeros(64)]), MLA_HEADS),
    ])
    return win, wq, wk, wv, pin, pq, pk, w_out.astype(_MX)


def _odd_weights(w_in, q_g, k_g, w_out):
    o1 = SPA_HEADS * SPA_HD
    o2, o3 = o1 + SPA_HD, o1 + 2 * SPA_HD
    o4 = o3 + IDX_HEADS * IDX_HD
    o5 = o4 + IDX_HD
    d = w_in.shape[0]
    k, v, ki = w_in[:, o1:o2], w_in[:, o2:o3], w_in[:, o4:o5]
    win = jnp.concatenate([w_in[:, :o1], k, k, v, v, w_in[:, o3:o4], ki, ki, w_in[:, o5:],
                           jnp.zeros((d, LANES - IDX_HEADS), _F32)], axis=1).astype(_MX)
    ones = jnp.ones((ODD_IN_PAD,), _F32)
    gain = jnp.concatenate([jnp.tile(q_g, SPA_HEADS), k_g, k_g, jnp.ones((ODD_IN_PAD - o1 - 2 * SPA_HD,), _F32)])
    pin = jnp.stack([ones / SPA_HD, gain])
    return win, pin, w_out.astype(_MX)


def kernel(x, mem, positions, ffn1_norm, ffn1_w13, ffn1_w2, mix_norm, xattn_norm, mem_norm,
           xattn_wq, xattn_wkv, xattn_q_gain, xattn_k_gain, xattn_wo, ffn2_norm, ffn2_w13, ffn2_w2,
           even_w_in, mla_q_lora_norm, mla_kv_lora_norm, mla_w_uq, mla_w_ukv, mla_q_gain, mla_k_gain,
           dil_q_gain, dil_k_gain, even_w_out, odd_w_in, sparse_q_gain, sparse_k_gain, odd_w_out):
    batch, seq, d = x.shape
    mem_len = mem.shape[1]
    n = batch * seq
    c64, n64, cm, nm = _tables(positions)
    s64, s128, s256 = _seg_matrix(64), _seg_matrix(128), _seg_matrix(256)
    bf = lambda a: a.astype(_MX)
    row3 = lambda a: a.reshape(a.shape[0], 1, a.shape[1])
    f1w13, f1w2, f2w13, f2w2 = bf(ffn1_w13), bf(ffn1_w2), bf(ffn2_w13), bf(ffn2_w2)
    f1g, f2g = row3(ffn1_norm), row3(ffn2_norm)
    xwq, xwkv, xwo = bf(xattn_wq), bf(xattn_wkv), bf(xattn_wo)

    xs = x.reshape(n, d)
    mems = mem.reshape(batch * mem_len, d)
    for i in range(DEPTH):
        j = i // 2
        xs = _ffn(xs, f1g, f1w13, f1w2, i)
        g_mix = mix_norm[i][None, :]
        if i % 2 == 0:
            win, wq, wk, wv, pin, pq, pk, wout = _even_weights(
                even_w_in[j], mla_q_lora_norm[j], mla_kv_lora_norm[j], mla_w_uq[j], mla_w_ukv[j],
                mla_q_gain[j], mla_k_gain[j], dil_q_gain[j], dil_k_gain[j], even_w_out[j])
            qm, km, vm, dil = _even_proj(xs, g_mix, win, wq, wk, wv, pin, pq, pk,
                                         (s64, s128, s256), (c64, n64, cm, nm))
            mla = _mla_attn(qm, km, vm, batch, seq)
            outs, lses = [], []
            for grp, (window, dilation) in enumerate(DIL_PATTERNS):
                o_g, lse_g = _dil_attn(dil, grp, dilation, batch, seq)
                outs.append(o_g)
                lses.append(lse_g)
            xs = _even_out(xs, mla, outs, lses, wout)
        else:
            win, pin, wout = _odd_weights(odd_w_in[j], sparse_q_gain[j], sparse_k_gain[j], odd_w_out[j])
            q, kv, qi, ki, wit = _odd_proj(xs, g_mix, win, pin, s64, (c64, n64))
            att = _sparse_attn(q, kv, qi, ki, wit, batch, seq)
            xs = _proj_residual(xs, att, wout)
        mk, mv = _mem_kv(mems, mem_norm[i][None, :], xwkv[i], xattn_k_gain[i][None, :])
        xs = _xattn(xs, xattn_norm[i][None, :], xwq[i], xattn_q_gain[i][None, :], mk, mv, xwo[i],
                    batch, seq, mem_len)
        xs = _ffn(xs, f2g, f2w13, f2w2, i)
    return xs.reshape(batch, seq, d)
```

```python
import functools
import math

import jax
import jax.numpy as jnp
import numpy as np
from jax import lax
from jax.experimental import pallas as pl
from jax.experimental.pallas import tpu as pltpu

D_MODEL = 1024
DEPTH = 4
EPS = 1e-6
D_FF = 2816
ROPE_THETA = 10000.0
MLA_HEADS, MLA_NOPE, MLA_ROPE, MLA_V = 8, 64, 32, 64
MLA_Q_LORA, MLA_KV_LORA = 256, 128
DIL_PATTERNS = ((128, 1), (512, 4), (2048, 16))
DIL_HEADS, DIL_HD = 4, 64
SPA_HEADS, SPA_HD = 16, 64
IDX_HEADS, IDX_HD = 8, 64
TOPK_MAX = 256
X_HEADS = 4
X_HD = D_MODEL // X_HEADS

LANES = 128
MXU_WIDTH = 256
VMEM_LIMIT_BYTES = 56 * 1024 * 1024

_MX = jnp.bfloat16
_F32 = jnp.float32
_NEG = -1e30


def _dot(a, b):
    return jnp.dot(a, b, preferred_element_type=_F32)


def _dot_nt(a, b):
    return lax.dot_general(a, b, (((1,), (1,)), ((), ())), preferred_element_type=_F32)


def _params(sem, flags=None):
    return pltpu.CompilerParams(dimension_semantics=sem, vmem_limit_bytes=VMEM_LIMIT_BYTES, flags=flags)


def _row_rmsnorm(x, g):
    return x * lax.rsqrt(jnp.mean(x * x, axis=-1, keepdims=True) + EPS) * g


def _seg_sumsq(y, seg_ref):
    y2 = y * y
    hi = y2.astype(_MX)
    lo = (y2 - hi.astype(_F32)).astype(_MX)
    s = seg_ref[...]
    return _dot(hi, s) + _dot(lo, s)


def _swap_lanes(z, d):
    lane = lax.broadcasted_iota(jnp.int32, z.shape, 1)
    fwd = pltpu.roll(z, LANES - d, 1)
    bwd = pltpu.roll(z, d, 1)
    return jnp.where((lane & d) == 0, fwd, bwd)


def _rope_tile(z, d, cos, sin_signed):
    return z * cos + _swap_lanes(z, d) * sin_signed


def _run_pipelined(jobs):
    y_next = jobs[0][0]()
    for i, (_, epilogue) in enumerate(jobs):
        y = y_next
        if i + 1 < len(jobs):
            y_next = jobs[i + 1][0]()
        epilogue(y)


FFN_TM = 512
FFN_TF = 1408
FFN_NK = D_FF // FFN_TF


def _ffn_kernel(x_ref, g_ref, wg_ref, wu_ref, w2_ref, o_ref, h_scr, acc_scr):
    k = pl.program_id(1)

    @pl.when(k == 0)
    def _():
        h_scr[...] = _row_rmsnorm(x_ref[...], g_ref[...]).astype(_MX)
        acc_scr[...] = jnp.zeros_like(acc_scr)

    h = h_scr[...]
    gate = _dot(h, wg_ref[...])
    up = _dot(h, wu_ref[...])
    act = (jax.nn.silu(gate) * up).astype(_MX)
    acc_scr[...] += _dot(act, w2_ref[...])

    @pl.when(k == FFN_NK - 1)
    def _():
        o_ref[...] = x_ref[...] + 0.5 * acc_scr[...]


def _ffn(x, g, w13, w2, layer):
    n = x.shape[0]
    return pl.pallas_call(
        _ffn_kernel,
        grid=(n // FFN_TM, FFN_NK),
        in_specs=[
            pl.BlockSpec((FFN_TM, D_MODEL), lambda i, k: (i, 0)),
            pl.BlockSpec((None, 1, D_MODEL), lambda i, k: (layer, 0, 0)),
            pl.BlockSpec((None, D_MODEL, FFN_TF), lambda i, k: (layer, 0, k)),
            pl.BlockSpec((None, D_MODEL, FFN_TF), lambda i, k: (layer, 0, FFN_NK + k)),
            pl.BlockSpec((None, FFN_TF, D_MODEL), lambda i, k: (layer, k, 0)),
        ],
        out_specs=pl.BlockSpec((FFN_TM, D_MODEL), lambda i, k: (i, 0)),
        out_shape=jax.ShapeDtypeStruct(x.shape, _F32),
        scratch_shapes=[pltpu.VMEM((FFN_TM, D_MODEL), _MX), pltpu.VMEM((FFN_TM, D_MODEL), _F32)],
        compiler_params=_params(("parallel", "arbitrary")),
        name="ffn",
    )(x, g, w13, w13, w2)


PROJ_TM = 256
EVEN_IN_PAD = 2816
DIL_COLS = 2304


def _even_proj_kernel(x_ref, g_ref, win_ref, wq_ref, wk_ref, wv_ref, pin_ref, pq_ref, pk_ref,
                      s64_ref, s128_ref, s256_ref, c64_ref, n64_ref, cm_ref, nm_ref,
                      qm_ref, km_ref, vt_ref, dil_ref):
    h = _row_rmsnorm(x_ref[...], g_ref[...]).astype(_MX)
    c64, n64, cm, nm = c64_ref[...], n64_ref[...], cm_ref[...], nm_ref[...]

    def normed(y, seg_ref, p_ref, c0):
        ss = _seg_sumsq(y, seg_ref)
        return y * lax.rsqrt(ss * p_ref[0:1, c0:c0 + MXU_WIDTH] + EPS) * p_ref[1:2, c0:c0 + MXU_WIDTH]

    tiles = lambda c0: [(t, slice(c0 + LANES * t, c0 + LANES * (t + 1))) for t in range(2)]
    latent = {}
    jobs = []

    def epi_cq(y):
        latent["cq"] = normed(y, s256_ref, pin_ref, 0).astype(_MX)
    jobs.append((lambda: _dot(h, win_ref[:, 0:256]), epi_cq))

    def epi_ckv(y):
        z = normed(y, s128_ref, pin_ref, 256)
        latent["ckv"] = z[:, :LANES].astype(_MX)
        latent["k_rope"] = _rope_tile(z[:, LANES:], MLA_ROPE // 2, cm, nm)
    jobs.append((lambda: _dot(h, win_ref[:, 256:512]), epi_ckv))

    for j in range(DIL_COLS // MXU_WIDTH):
        def epi_dil(y, j=j):
            if j % 3 < 2:
                z = normed(y, s64_ref, pin_ref, 512 + MXU_WIDTH * j)
                for t, cols in tiles(MXU_WIDTH * j):
                    dil_ref[:, cols] = _rope_tile(z[:, LANES * t:LANES * (t + 1)], DIL_HD // 2, c64, n64)
            else:
                dil_ref[:, MXU_WIDTH * j:MXU_WIDTH * (j + 1)] = y
        jobs.append((lambda j=j: _dot(h, win_ref[:, 512 + MXU_WIDTH * j:512 + MXU_WIDTH * (j + 1)]), epi_dil))

    for j in range(MLA_HEADS * LANES // MXU_WIDTH):
        c0 = MXU_WIDTH * j
        def epi_q(y, c0=c0):
            z = normed(y, s64_ref, pq_ref, c0)
            for t, cols in tiles(c0):
                qm_ref[:, cols] = _rope_tile(z[:, LANES * t:LANES * (t + 1)], MLA_ROPE // 2, cm, nm).astype(_MX)
        jobs.append((lambda c0=c0: _dot(latent["cq"], wq_ref[:, c0:c0 + MXU_WIDTH]), epi_q))

        def epi_k(y, c0=c0):
            z = normed(y, s64_ref, pk_ref, c0)
            for t, cols in tiles(c0):
                km_ref[:, cols] = (z[:, LANES * t:LANES * (t + 1)] + latent["k_rope"]).astype(_MX)
        jobs.append((lambda c0=c0: _dot(latent["ckv"], wk_ref[:, c0:c0 + MXU_WIDTH]), epi_k))

    for j in range(MLA_HEADS * MLA_V // MXU_WIDTH):
        c0 = MXU_WIDTH * j
        def epi_v(y, c0=c0):
            for t, cols in tiles(c0):
                vt_ref[cols, :] = y[:, LANES * t:LANES * (t + 1)].T.astype(_MX)
        jobs.append((lambda c0=c0: _dot(latent["ckv"], wv_ref[:, c0:c0 + MXU_WIDTH]), epi_v))
    _run_pipelined(jobs)


def _const_spec(shape):
    nd = len(shape)
    return pl.BlockSpec(shape, lambda i: (0,) * nd)


def _even_proj(x, g, win, wq, wk, wv, pin, pq, pk, segs, tabs):
    n = x.shape[0]
    tm = PROJ_TM
    row = lambda w: pl.BlockSpec((tm, w), lambda i: (i, 0))
    s64, s128, s256 = segs
    c64, n64, cm, nm = tabs
    return pl.pallas_call(
        _even_proj_kernel,
        grid=(n // tm,),
        in_specs=[row(D_MODEL), _const_spec(g.shape), _const_spec(win.shape), _const_spec(wq.shape),
                  _const_spec(wk.shape), _const_spec(wv.shape), _const_spec(pin.shape),
                  _const_spec(pq.shape), _const_spec(pk.shape), _const_spec(s64.shape),
                  _const_spec(s128.shape), _const_spec(s256.shape),
                  row(LANES), row(LANES), row(LANES), row(LANES)],
        out_specs=[row(MLA_HEADS * LANES), row(MLA_HEADS * LANES),
                   pl.BlockSpec((None, MLA_HEADS * MLA_V, tm), lambda i: (i, 0, 0)), row(DIL_COLS)],
        out_shape=[jax.ShapeDtypeStruct((n, MLA_HEADS * LANES), _MX),
                   jax.ShapeDtypeStruct((n, MLA_HEADS * LANES), _MX),
                   jax.ShapeDtypeStruct((n // tm, MLA_HEADS * MLA_V, tm), _MX),
                   jax.ShapeDtypeStruct((n, DIL_COLS), _F32)],
        compiler_params=_params(("parallel",)),
        name="even_proj",
    )(x, g, win, wq, wk, wv, pin, pq, pk, s64, s128, s256, c64, n64, cm, nm)


MLA_TQ = 256
MLA_TK = 256


def _mla_attn_kernel(q_ref, k_ref, vt_ref, o_ref, m_scr, l_scr, acc_scr):
    qi = pl.program_id(1)
    scale = (MLA_NOPE + MLA_ROPE) ** -0.5
    m_scr[...] = jnp.full_like(m_scr, _NEG)
    l_scr[...] = jnp.zeros_like(l_scr)
    acc_scr[...] = jnp.zeros_like(acc_scr)
    kpos = lax.broadcasted_iota(jnp.int32, (MLA_TK, MLA_TQ), 0)
    qpos = qi * MLA_TQ + lax.broadcasted_iota(jnp.int32, (MLA_TK, MLA_TQ), 1)

    def chunk(c, carry):
        k0 = pl.multiple_of(c * MLA_TK, MLA_TK)
        visible = (k0 + kpos) <= qpos
        scores = lambda hd: _dot_nt(k_ref[pl.ds(k0, MLA_TK), LANES * hd:LANES * (hd + 1)],
                                    q_ref[:, LANES * hd:LANES * (hd + 1)])
        st_next = scores(0)
        for hd in range(MLA_HEADS):
            st = jnp.where(visible, st_next * scale, _NEG)
            if hd + 1 < MLA_HEADS:
                st_next = scores(hd + 1)
            m_old = m_scr[hd]
            m_new = jnp.maximum(m_old, jnp.max(st, axis=0, keepdims=True))
            alpha = jnp.exp(m_old - m_new)
            p = jnp.exp(st - m_new)
            l_scr[hd] = alpha * l_scr[hd] + jnp.sum(p, axis=0, keepdims=True)
            vt = vt_ref[c, MLA_V * hd:MLA_V * (hd + 1), :]
            acc_scr[hd] = alpha * acc_scr[hd] + _dot(vt, p.astype(_MX))
            m_scr[hd] = m_new
        return carry

    lax.fori_loop(0, qi + 1, chunk, 0)
    ot = jnp.concatenate([acc_scr[hd] / l_scr[hd] for hd in range(MLA_HEADS)], axis=0)
    o_ref[...] = ot.T.astype(o_ref.dtype)


def _mla_attn(qm, km, vt, batch, seq):
    n = qm.shape[0]
    nq = seq // MLA_TQ
    wv = MLA_HEADS * MLA_V
    return pl.pallas_call(
        _mla_attn_kernel,
        grid=(batch, nq),
        in_specs=[
            pl.BlockSpec((MLA_TQ, MLA_HEADS * LANES), lambda b, i: (b * nq + i, 0)),
            pl.BlockSpec((seq, MLA_HEADS * LANES), lambda b, i: (b, 0)),
            pl.BlockSpec((seq // MLA_TK, wv, MLA_TK), lambda b, i: (b, 0, 0)),
        ],
        out_specs=pl.BlockSpec((MLA_TQ, wv), lambda b, i: (b * nq + i, 0)),
        out_shape=jax.ShapeDtypeStruct((n, wv), _MX),
        scratch_shapes=[pltpu.VMEM((MLA_HEADS, 1, MLA_TQ), _F32), pltpu.VMEM((MLA_HEADS, 1, MLA_TQ), _F32),
                        pltpu.VMEM((MLA_HEADS, MLA_V, MLA_TQ), _F32)],
        compiler_params=_params(("parallel", "arbitrary")),
        name="mla_attn",
    )(qm, km, vt)


DIL_BLK = 128
DIL_UNROLL = 2


def _dil_attn_kernel(*refs, seq):
    n_grp = len(DIL_PATTERNS)
    qkv = [refs[3 * g:3 * g + 3] for g in range(n_grp)]
    out_ref = refs[3 * n_grp]
    o_slabs = refs[3 * n_grp + 1:3 * n_grp + 1 + n_grp]
    l_slabs = refs[3 * n_grp + 1 + n_grp:]
    scale = DIL_HD ** -0.5
    kidx = lax.broadcasted_iota(jnp.int32, (DIL_BLK, DIL_BLK), 0)
    qidx = lax.broadcasted_iota(jnp.int32, (DIL_BLK, DIL_BLK), 1)
    lane = lax.broadcasted_iota(jnp.int32, (DIL_BLK, LANES), 1)
    dim = lax.broadcasted_iota(jnp.int32, (LANES, DIL_BLK), 0)
    cur_ok = kidx <= qidx

    for g, (window, d) in enumerate(DIL_PATTERNS):
        q_ref, k_ref, v_ref = qkv[g]
        nb = seq // d // DIL_BLK

        def units(it, carry, d=d, nb=nb, q_ref=q_ref, k_ref=k_ref, v_ref=v_ref, g=g):
            staged = []
            for u in range(DIL_UNROLL):
                idx = it * DIL_UNROLL + u
                r = idx // nb
                n = idx % nb
                rows = pl.ds(r + d * DIL_BLK * n, DIL_BLK, stride=d)
                prev = pl.ds(r + d * DIL_BLK * jnp.maximum(n - 1, 0), DIL_BLK, stride=d)
                prev_ok = kidx >= qidx + jnp.where(n > 0, 0, DIL_BLK)
                q2 = q_ref[rows, :] * scale
                kc = k_ref[rows, :].astype(_MX)
                kp = k_ref[prev, :].astype(_MX)
                halves = []
                for half in range(2):
                    mine = (lane < DIL_HD) if half == 0 else (lane >= DIL_HD)
                    qh = jnp.where(mine, q2, 0.0).astype(_MX)
                    halves.append((_dot_nt(kc, qh), _dot_nt(kp, qh)))
                staged.append((rows, prev, prev_ok, halves))
            for rows, prev, prev_ok, halves in staged:
                vct = v_ref[rows, :].T.astype(_MX)
                vpt = v_ref[prev, :].T.astype(_MX)
                ots, lses = [], []
                for sc, sp in halves:
                    sc = jnp.where(cur_ok, sc, _NEG)
                    sp = jnp.where(prev_ok, sp, _NEG)
                    m = jnp.maximum(jnp.max(sc, axis=0, keepdims=True), jnp.max(sp, axis=0, keepdims=True))
                    pc = jnp.exp(sc - m)
                    pp = jnp.exp(sp - m)
                    l = jnp.sum(pc, axis=0, keepdims=True) + jnp.sum(pp, axis=0, keepdims=True)
                    ots.append((_dot(vct, pc.astype(_MX)) + _dot(vpt, pp.astype(_MX))) / l)
                    lses.append(m + jnp.log(l))
                ot = jnp.where(dim < DIL_HD, ots[0], ots[1])
                lt = jnp.where(dim < DIL_HD, lses[0], lses[1])
                o_slabs[g][rows, :] = ot.T
                l_slabs[g][rows, :] = lt.T
            return carry

        lax.fori_loop(0, d * nb // DIL_UNROLL, units, 0)

    ls = [l_slabs[g][...] for g in range(n_grp)]
    m = functools.reduce(jnp.maximum, ls)
    es = [jnp.exp(l - m) for l in ls]
    num = sum(e * o_slabs[g][...] for g, e in enumerate(es))
    out_ref[...] = (num / sum(es)).astype(out_ref.dtype)


def _dil_attn(dil, batch, seq):
    n = dil.shape[0]
    pairs = DIL_HEADS // 2
    spec = lambda g, part: pl.BlockSpec((seq, LANES), lambda b, p: (b, (3 * g + part) * pairs + p))
    in_specs = [spec(g, part) for g in range(len(DIL_PATTERNS)) for part in range(3)]
    return pl.pallas_call(
        functools.partial(_dil_attn_kernel, seq=seq),
        grid=(batch, pairs),
        in_specs=in_specs,
        out_specs=pl.BlockSpec((seq, LANES), lambda b, p: (b, p)),
        out_shape=jax.ShapeDtypeStruct((n, DIL_HEADS * DIL_HD), _MX),
        scratch_shapes=[pltpu.VMEM((seq, LANES), _F32)] * (2 * len(DIL_PATTERNS)),
        compiler_params=_params(("parallel", "parallel")),
        name="dil_attn",
    )(*([dil] * len(in_specs)))


OUT_TM = 512


def _even_out_kernel(x_ref, mla_ref, dil_ref, w_ref, out_ref):
    nm = MLA_HEADS * MLA_V
    mix = _dot(mla_ref[...], w_ref[0:nm, :]) + _dot(dil_ref[...], w_ref[nm:, :])
    out_ref[...] = x_ref[...] + mix


def _even_out(x, mla, dil, w):
    n = x.shape[0]
    tm = OUT_TM
    row = lambda wd: pl.BlockSpec((tm, wd), lambda i: (i, 0))
    return pl.pallas_call(
        _even_out_kernel,
        grid=(n // tm,),
        in_specs=[row(D_MODEL), row(MLA_HEADS * MLA_V), row(DIL_HEADS * DIL_HD), _const_spec(w.shape)],
        out_specs=row(D_MODEL),
        out_shape=jax.ShapeDtypeStruct(x.shape, _F32),
        compiler_params=_params(("parallel",)),
        name="even_out",
    )(x, mla, dil, w)


ODD_IN_PAD = 2048


def _odd_proj_kernel(x_ref, g_ref, win_ref, pin_ref, s64_ref, c64_ref, n64_ref,
                     q_ref, kk_ref, vt_ref, qi_ref, ki_ref, wit_ref):
    h = _row_rmsnorm(x_ref[...], g_ref[...]).astype(_MX)
    c64, n64 = c64_ref[...], n64_ref[...]

    def normed(y, c0):
        ss = _seg_sumsq(y, s64_ref)
        return y * lax.rsqrt(ss * pin_ref[0:1, c0:c0 + MXU_WIDTH] + EPS) * pin_ref[1:2, c0:c0 + MXU_WIDTH]

    def rope64(t):
        return _rope_tile(t, SPA_HD // 2, c64, n64)

    nq = SPA_HEADS * SPA_HD
    c_qi = nq + MXU_WIDTH
    c_ki = c_qi + IDX_HEADS * IDX_HD
    product = lambda c0: (lambda: _dot(h, win_ref[:, c0:c0 + MXU_WIDTH]))
    jobs = []
    for j in range(nq // MXU_WIDTH):
        def epi_q(y, c0=MXU_WIDTH * j):
            z = normed(y, c0)
            for t in range(2):
                q_ref[:, c0 + LANES * t:c0 + LANES * (t + 1)] = rope64(z[:, LANES * t:LANES * (t + 1)]).astype(_MX)
        jobs.append((product(MXU_WIDTH * j), epi_q))

    def epi_kv(y):
        z = normed(y, nq)
        kk_ref[...] = rope64(z[:, :LANES]).astype(_MX)
        vt_ref[...] = y[:, LANES:].T.astype(_MX)
    jobs.append((product(nq), epi_kv))

    for j in range(IDX_HEADS * IDX_HD // MXU_WIDTH):
        def epi_qi(y, o0=MXU_WIDTH * j):
            for t in range(2):
                qi_ref[:, o0 + LANES * t:o0 + LANES * (t + 1)] = rope64(y[:, LANES * t:LANES * (t + 1)]).astype(_MX)
        jobs.append((product(c_qi + MXU_WIDTH * j), epi_qi))

    def epi_ki(y):
        ki_ref[...] = rope64(y[:, :LANES]).astype(_MX)
        wi = (y[:, LANES:] * (IDX_HEADS ** -0.5)) * (IDX_HD ** -0.5)
        wit_ref[...] = wi.T[0:IDX_HEADS, :]
    jobs.append((product(c_ki), epi_ki))
    _run_pipelined(jobs)


def _odd_proj(x, g, win, pin, s64, tabs):
    n = x.shape[0]
    tm = PROJ_TM
    row = lambda w: pl.BlockSpec((tm, w), lambda i: (i, 0))
    c64, n64 = tabs
    return pl.pallas_call(
        _odd_proj_kernel,
        grid=(n // tm,),
        in_specs=[row(D_MODEL), _const_spec(g.shape), _const_spec(win.shape), _const_spec(pin.shape),
                  _const_spec(s64.shape), row(LANES), row(LANES)],
        out_specs=[row(SPA_HEADS * SPA_HD), row(LANES),
                   pl.BlockSpec((None, LANES, tm), lambda i: (i, 0, 0)),
                   row(IDX_HEADS * IDX_HD), row(LANES),
                   pl.BlockSpec((IDX_HEADS, tm), lambda i: (0, i))],
        out_shape=[jax.ShapeDtypeStruct((n, SPA_HEADS * SPA_HD), _MX),
                   jax.ShapeDtypeStruct((n, LANES), _MX),
                   jax.ShapeDtypeStruct((n // tm, LANES, tm), _MX),
                   jax.ShapeDtypeStruct((n, IDX_HEADS * IDX_HD), _MX),
                   jax.ShapeDtypeStruct((n, LANES), _MX),
                   jax.ShapeDtypeStruct((IDX_HEADS, n), _F32)],
        compiler_params=_params(("parallel",)),
        name="odd_proj",
    )(x, g, win, pin, s64, c64, n64)


SPA_TQ = 128
SPA_TK = 256
_INT_MIN = -2 ** 31
assert SPA_TK == PROJ_TM and MLA_TK == PROJ_TM


def _sparse_attn_kernel(q_ref, kk_ref, vt_ref, qi_ref, ki_ref, wit_ref, o_ref,
                        key_scr, qs_scr, m_scr, acc_scr, *, n_keep):
    qb = pl.program_id(1)
    n_chunks = qb // (SPA_TK // SPA_TQ) + 1
    lane = lax.broadcasted_iota(jnp.int32, (SPA_TQ, LANES), 1)
    left = lane < SPA_HD

    def stack_heads(ref, n_pairs):
        parts = []
        for pr in range(n_pairs):
            t = ref[:, LANES * pr:LANES * (pr + 1)]
            parts.append(jnp.where(left, t, jnp.zeros_like(t)))
            parts.append(jnp.where(left, jnp.zeros_like(t), t))
        return jnp.concatenate(parts, axis=0)

    qi_stack = stack_heads(qi_ref, IDX_HEADS // 2)
    wit = wit_ref[...]
    krow = lax.broadcasted_iota(jnp.int32, (SPA_TK, SPA_TQ), 0)
    qpos = qb * SPA_TQ + lax.broadcasted_iota(jnp.int32, (SPA_TK, SPA_TQ), 1)

    def score_chunk(c, carry):
        k0 = pl.multiple_of(c * SPA_TK, SPA_TK)
        logits = _dot_nt(ki_ref[pl.ds(k0, SPA_TK), :], qi_stack)
        sc = jnp.zeros((SPA_TK, SPA_TQ), _F32)
        for hd in range(IDX_HEADS):
            sc = sc + wit[hd:hd + 1, :] * jnp.maximum(logits[:, SPA_TQ * hd:SPA_TQ * (hd + 1)], 0.0)
        sc = jnp.where((k0 + krow) <= qpos, sc, -jnp.inf)
        bits = pltpu.bitcast(sc, jnp.int32)
        key_scr[c] = bits ^ ((bits >> 31) & jnp.int32(0x7FFFFFFF))
        return carry

    lax.fori_loop(0, n_chunks, score_chunk, 0)

    def count(pred):
        def body(c, acc):
            hit = pred(key_scr[c], c * SPA_TK)
            return acc + jnp.sum(hit.astype(jnp.int32).reshape(SPA_TK // 8, 8, SPA_TQ), axis=0)
        acc = lax.fori_loop(0, n_chunks, body, jnp.zeros((8, SPA_TQ), jnp.int32))
        return jnp.sum(acc, axis=0, keepdims=True)

    def refine(t, cand):
        return jnp.where(count(lambda k, k0: k >= cand) >= n_keep, cand, t)

    t = jnp.full((1, SPA_TQ), _INT_MIN, jnp.int32)
    t = refine(t, jnp.zeros((1, SPA_TQ), jnp.int32))
    t = lax.fori_loop(0, 31, lambda i, t: refine(t, t | (jnp.int32(1) << (30 - i))), t)
    need = n_keep - count(lambda k, k0: k > t)

    def refine_pos(j, cand):
        below = count(lambda k, k0: (k == t) & ((k0 + krow) < cand))
        return jnp.where(below < need, cand, j)

    n_bits = (key_scr.shape[0] * SPA_TK).bit_length()
    last = lax.fori_loop(0, n_bits, lambda i, j: refine_pos(j, j | (jnp.int32(1) << (n_bits - 1 - i))),
                         jnp.zeros((1, SPA_TQ), jnp.int32))

    qs_scr[...] = stack_heads(q_ref, SPA_HEADS // 2) * (SPA_HD ** -0.5)
    m_scr[...] = jnp.full_like(m_scr, _NEG)
    acc_scr[...] = jnp.zeros_like(acc_scr)
    vrow = lax.broadcasted_iota(jnp.int32, (LANES, SPA_TK), 0)
    pair_w = 2 * SPA_TQ

    def attend_chunk(c, carry):
        k0 = pl.multiple_of(c * SPA_TK, SPA_TK)
        keys = key_scr[c]
        kpos = k0 + krow
        chosen = ((keys > t) | ((keys == t) & (kpos <= last))) & (kpos <= qpos)
        bias = jnp.where(chosen, 0.0, _NEG)
        bias2 = jnp.concatenate([bias, bias], axis=1)
        kk = kk_ref[pl.ds(k0, SPA_TK), :]
        vt = vt_ref[c]
        v1t = jnp.where(vrow < SPA_HD, vt, jnp.ones_like(vt))
        n_pairs = SPA_HEADS // 2
        scores = lambda pr: _dot_nt(kk, qs_scr[pair_w * pr:pair_w * (pr + 1), :])
        st_next = scores(0)
        for pr in range(n_pairs):
            cols = slice(pair_w * pr, pair_w * (pr + 1))
            st = st_next + bias2
            if pr + 1 < n_pairs:
                st_next = scores(pr + 1)
            m_old = m_scr[:, cols]
            m_new = jnp.maximum(m_old, jnp.max(st, axis=0, keepdims=True))
            p = jnp.exp(st - m_new)
            acc_scr[:, cols] = jnp.exp(m_old - m_new) * acc_scr[:, cols] + _dot(v1t, p.astype(_MX))
            m_scr[:, cols] = m_new
        return carry

    lax.fori_loop(0, n_chunks, attend_chunk, 0)
    for pr in range(SPA_HEADS // 2):
        a = acc_scr[:, pair_w * pr:pair_w * pr + SPA_TQ]
        b = acc_scr[:, pair_w * pr + SPA_TQ:pair_w * (pr + 1)]
        tile_t = jnp.concatenate([a[:SPA_HD] / a[SPA_HD:SPA_HD + 1], b[:SPA_HD] / b[SPA_HD:SPA_HD + 1]], axis=0)
        o_ref[:, LANES * pr:LANES * (pr + 1)] = tile_t.T.astype(o_ref.dtype)


def _sparse_attn(q, kk, vt, qi, ki, wit, batch, seq):
    n = q.shape[0]
    nq = seq // SPA_TQ
    n_keep = min(TOPK_MAX, seq // 4)
    return pl.pallas_call(
        functools.partial(_sparse_attn_kernel, n_keep=n_keep),
        grid=(batch, nq),
        in_specs=[
            pl.BlockSpec((SPA_TQ, SPA_HEADS * SPA_HD), lambda b, i: (b * nq + i, 0)),
            pl.BlockSpec((seq, LANES), lambda b, i: (b, 0)),
            pl.BlockSpec((seq // SPA_TK, LANES, SPA_TK), lambda b, i: (b, 0, 0)),
            pl.BlockSpec((SPA_TQ, IDX_HEADS * IDX_HD), lambda b, i: (b * nq + i, 0)),
            pl.BlockSpec((seq, LANES), lambda b, i: (b, 0)),
            pl.BlockSpec((IDX_HEADS, SPA_TQ), lambda b, i: (0, b * nq + i)),
        ],
        out_specs=pl.BlockSpec((SPA_TQ, SPA_HEADS * SPA_HD), lambda b, i: (b * nq + i, 0)),
        out_shape=jax.ShapeDtypeStruct((n, SPA_HEADS * SPA_HD), _MX),
        scratch_shapes=[pltpu.VMEM((seq // SPA_TK, SPA_TK, SPA_TQ), jnp.int32),
                        pltpu.VMEM((SPA_HEADS * SPA_TQ, LANES), _MX),
                        pltpu.VMEM((1, SPA_HEADS * SPA_TQ), _F32),
                        pltpu.VMEM((LANES, SPA_HEADS * SPA_TQ), _F32)],
        compiler_params=_params(("parallel", "arbitrary")),
        name="sparse_attn",
    )(q, kk, vt, qi, ki, wit)


def _proj_residual_kernel(x_ref, a_ref, w_ref, out_ref):
    out_ref[...] = x_ref[...] + _dot(a_ref[...], w_ref[...])


def _proj_residual(x, a, w):
    n = x.shape[0]
    tm = OUT_TM
    return pl.pallas_call(
        _proj_residual_kernel,
        grid=(n // tm,),
        in_specs=[pl.BlockSpec((tm, D_MODEL), lambda i: (i, 0)),
                  pl.BlockSpec((tm, a.shape[1]), lambda i: (i, 0)), _const_spec(w.shape)],
        out_specs=pl.BlockSpec((tm, D_MODEL), lambda i: (i, 0)),
        out_shape=jax.ShapeDtypeStruct(x.shape, _F32),
        compiler_params=_params(("parallel",)),
        name="odd_out",
    )(x, a, w)


def _mem_kv_kernel(mem_ref, g_ref, w_ref, kg_ref, k_ref, v_ref):
    h = _row_rmsnorm(mem_ref[...], g_ref[...]).astype(_MX)
    kg = kg_ref[...]
    for hd in range(X_HEADS):
        cols = slice(X_HD * hd, X_HD * (hd + 1))
        y = _dot(h, w_ref[:, cols])
        k_ref[:, cols] = _row_rmsnorm(y, kg).astype(_MX)
    for hd in range(X_HEADS):
        cols = slice(X_HD * hd, X_HD * (hd + 1))
        v_ref[:, cols] = _dot(h, w_ref[:, D_MODEL + X_HD * hd:D_MODEL + X_HD * (hd + 1)]).astype(_MX)


def _mem_kv(mem, g, wkv, kg):
    n, m = mem.shape[0], 256
    row = pl.BlockSpec((m, D_MODEL), lambda i: (i, 0))
    return pl.pallas_call(
        _mem_kv_kernel,
        grid=(n // m,),
        in_specs=[row, _const_spec(g.shape), _const_spec(wkv.shape), _const_spec(kg.shape)],
        out_specs=[row, row],
        out_shape=[jax.ShapeDtypeStruct((n, D_MODEL), _MX)] * 2,
        compiler_params=_params(("parallel",)),
        name="mem_kv",
    )(mem, g, wkv, kg)


XATT_TM = 512


def _xattn_kernel(x_ref, g_ref, wq_ref, qg_ref, k_ref, v_ref, wo_ref, out_ref, o_scr):
    x = x_ref[...]
    h = _row_rmsnorm(x, g_ref[...]).astype(_MX)
    qg = qg_ref[...]
    scale = X_HD ** -0.5
    heads = [slice(X_HD * hd, X_HD * (hd + 1)) for hd in range(X_HEADS)]
    qs = [_dot(h, wq_ref[:, cols]) for cols in heads]
    qs = [_row_rmsnorm(q, qg).astype(_MX) for q in qs]
    ss = [_dot_nt(q, k_ref[:, cols]) * scale for q, cols in zip(qs, heads)]
    for s, cols in zip(ss, heads):
        p = jnp.exp(s - jnp.max(s, axis=-1, keepdims=True))
        p = p / jnp.sum(p, axis=-1, keepdims=True)
        o_scr[:, cols] = _dot(p.astype(_MX), v_ref[:, cols]).astype(_MX)
    out_ref[...] = x + _dot(o_scr[...], wo_ref[...])


def _xattn(x, g, wq, qg, k, v, wo, batch, seq, mem_len):
    n = x.shape[0]
    tm = XATT_TM
    nt = seq // tm
    row = pl.BlockSpec((tm, D_MODEL), lambda b, i: (b * nt + i, 0))
    const = lambda a: pl.BlockSpec(a.shape, lambda b, i: (0,) * a.ndim)
    memspec = pl.BlockSpec((mem_len, D_MODEL), lambda b, i: (b, 0))
    return pl.pallas_call(
        _xattn_kernel,
        grid=(batch, nt),
        in_specs=[row, const(g), const(wq), const(qg), memspec, memspec, const(wo)],
        out_specs=row,
        out_shape=jax.ShapeDtypeStruct(x.shape, _F32),
        scratch_shapes=[pltpu.VMEM((tm, D_MODEL), _MX)],
        compiler_params=_params(("parallel", "parallel")),
        name="xattn",
    )(x, g, wq, qg, k, v, wo)


def _seg_matrix(seg):
    idx = np.arange(MXU_WIDTH) // seg
    return jnp.asarray(idx[:, None] == idx[None, :], _MX)


def _rope_tables(positions, dim):
    inv = jnp.exp(-math.log(ROPE_THETA) * jnp.arange(0, dim, 2, dtype=_F32) / dim)
    ang = positions.astype(_F32).reshape(-1)[:, None] * inv
    return jnp.cos(ang), jnp.sin(ang)


def _tables(positions):
    cos64, sin64 = _rope_tables(positions, DIL_HD)
    cos32, sin32 = _rope_tables(positions, MLA_ROPE)
    n = cos64.shape[0]
    c64 = jnp.tile(cos64, (1, 4))
    n64 = jnp.tile(jnp.concatenate([-sin64, sin64], axis=1), (1, 2))
    ones, zeros = jnp.ones((n, MLA_NOPE), _F32), jnp.zeros((n, MLA_NOPE), _F32)
    pad1, pad0 = jnp.ones((n, 32), _F32), jnp.zeros((n, 32), _F32)
    cm = jnp.concatenate([ones, cos32, cos32, pad1], axis=1)
    nm = jnp.concatenate([zeros, -sin32, sin32, pad0], axis=1)
    return c64, n64, cm, nm


def _even_weights(w_in, q_lora_g, kv_lora_g, w_uq, w_ukv, q_g, k_g, dq_g, dk_g, w_out):
    o1, o2, o3 = MLA_Q_LORA, MLA_Q_LORA + MLA_KV_LORA, MLA_Q_LORA + MLA_KV_LORA + MLA_ROPE
    d = w_in.shape[0]
    z = lambda w: jnp.zeros((d, w), _F32)
    win = jnp.concatenate([w_in[:, :o2], z(64), w_in[:, o2:o3], z(32), w_in[:, o3:]], axis=1).astype(_MX)
    ones = lambda w: jnp.ones((w,), _F32)
    zeros = lambda w: jnp.zeros((w,), _F32)
    grp_gain = jnp.concatenate([jnp.tile(dq_g, DIL_HEADS), jnp.tile(dk_g, DIL_HEADS), ones(256)])
    pin = jnp.stack([
        jnp.concatenate([ones(256) / 256, ones(128) / 128, ones(128) / MLA_ROPE, ones(DIL_COLS) / DIL_HD]),
        jnp.concatenate([q_lora_g, kv_lora_g, zeros(64), k_g[MLA_NOPE:], zeros(32), jnp.tile(grp_gain, 3)]),
    ])
    wq = jnp.pad(w_uq.reshape(MLA_Q_LORA, MLA_HEADS, MLA_NOPE + MLA_ROPE), ((0, 0), (0, 0), (0, 32)))
    wq = wq.reshape(MLA_Q_LORA, MLA_HEADS * LANES).astype(_MX)
    pq = jnp.stack([
        jnp.tile(jnp.concatenate([ones(64) / MLA_NOPE, ones(64) / MLA_ROPE]), MLA_HEADS),
        jnp.tile(jnp.concatenate([q_g, zeros(32)]), MLA_HEADS),
    ])
    ukv = w_ukv.reshape(MLA_KV_LORA, MLA_HEADS, MLA_NOPE + MLA_V)
    wk = jnp.pad(ukv[:, :, :MLA_NOPE], ((0, 0), (0, 0), (0, 64))).reshape(MLA_KV_LORA, MLA_HEADS * LANES).astype(_MX)
    wv = ukv[:, :, MLA_NOPE:].reshape(MLA_KV_LORA, MLA_HEADS * MLA_V).astype(_MX)
    pk = jnp.stack([
        jnp.tile(ones(LANES) / MLA_NOPE, MLA_HEADS),
        jnp.tile(jnp.concatenate([k_g[:MLA_NOPE], zeros(64)]), MLA_HEADS),
    ])
    return win, wq, wk, wv, pin, pq, pk, w_out.astype(_MX)


def _odd_weights(w_in, q_g, k_g, w_out):
    o1 = SPA_HEADS * SPA_HD
    o2, o3 = o1 + SPA_HD, o1 + 2 * SPA_HD
    o4 = o3 + IDX_HEADS * IDX_HD
    o5 = o4 + IDX_HD
    d = w_in.shape[0]
    k, v, ki = w_in[:, o1:o2], w_in[:, o2:o3], w_in[:, o4:o5]
    win = jnp.concatenate([w_in[:, :o1], k, k, v, v, w_in[:, o3:o4], ki, ki, w_in[:, o5:],
                           jnp.zeros((d, LANES - IDX_HEADS), _F32)], axis=1).astype(_MX)
    ones = jnp.ones((ODD_IN_PAD,), _F32)
    gain = jnp.concatenate([jnp.tile(q_g, SPA_HEADS), k_g, k_g, jnp.ones((ODD_IN_PAD - o1 - 2 * SPA_HD,), _F32)])
    pin = jnp.stack([ones / SPA_HD, gain])
    return win, pin, w_out.astype(_MX)


def kernel(x, mem, positions, ffn1_norm, ffn1_w13, ffn1_w2, mix_norm, xattn_norm, mem_norm,
           xattn_wq, xattn_wkv, xattn_q_gain, xattn_k_gain, xattn_wo, ffn2_norm, ffn2_w13, ffn2_w2,
           even_w_in, mla_q_lora_norm, mla_kv_lora_norm, mla_w_uq, mla_w_ukv, mla_q_gain, mla_k_gain,
           dil_q_gain, dil_k_gain, even_w_out, odd_w_in, sparse_q_gain, sparse_k_gain, odd_w_out):
    batch, seq, d = x.shape
    mem_len = mem.shape[1]
    n = batch * seq
    c64, n64, cm, nm = _tables(positions)
    s64, s128, s256 = _seg_matrix(64), _seg_matrix(128), _seg_matrix(256)
    bf = lambda a: a.astype(_MX)
    row3 = lambda a: a.reshape(a.shape[0], 1, a.shape[1])
    f1w13, f1w2, f2w13, f2w2 = bf(ffn1_w13), bf(ffn1_w2), bf(ffn2_w13), bf(ffn2_w2)
    f1g, f2g = row3(ffn1_norm), row3(ffn2_norm)
    xwq, xwkv, xwo = bf(xattn_wq), bf(xattn_wkv), bf(xattn_wo)

    xs = x.reshape(n, d)
    mems = mem.reshape(batch * mem_len, d)
    for i in range(DEPTH):
        j = i // 2
        xs = _ffn(xs, f1g, f1w13, f1w2, i)
        g_mix = mix_norm[i][None, :]
        if i % 2 == 0:
            win, wq, wk, wv, pin, pq, pk, wout = _even_weights(
                even_w_in[j], mla_q_lora_norm[j], mla_kv_lora_norm[j], mla_w_uq[j], mla_w_ukv[j],
                mla_q_gain[j], mla_k_gain[j], dil_q_gain[j], dil_k_gain[j], even_w_out[j])
            qm, km, vt, dil = _even_proj(xs, g_mix, win, wq, wk, wv, pin, pq, pk,
                                         (s64, s128, s256), (c64, n64, cm, nm))
            mla = _mla_attn(qm, km, vt, batch, seq)
            xs = _even_out(xs, mla, _dil_attn(dil, batch, seq), wout)
        else:
            win, pin, wout = _odd_weights(odd_w_in[j], sparse_q_gain[j], sparse_k_gain[j], odd_w_out[j])
            q, kk, vt, qi, ki, wit = _odd_proj(xs, g_mix, win, pin, s64, (c64, n64))
            att = _sparse_attn(q, kk, vt, qi, ki, wit, batch, seq)
            xs = _proj_residual(xs, att, wout)
        mk, mv = _mem_kv(mems, mem_norm[i][None, :], xwkv[i], xattn_k_gain[i][None, :])
        xs = _xattn(xs, xattn_norm[i][None, :], xwq[i], xattn_q_gain[i][None, :], mk, mv, xwo[i],
                    batch, seq, mem_len)
        xs = _ffn(xs, f2g, f2w13, f2w2, i)
    return xs.reshape(batch, seq, d)
```

```python
import functools
import math

import jax
import jax.numpy as jnp
import numpy as np
from jax import lax
from jax.experimental import pallas as pl
from jax.experimental.pallas import tpu as pltpu

D_MODEL = 1024
DEPTH = 4
EPS = 1e-6
D_FF = 2816
ROPE_THETA = 10000.0
MLA_HEADS, MLA_NOPE, MLA_ROPE, MLA_V = 8, 64, 32, 64
MLA_Q_LORA, MLA_KV_LORA = 256, 128
DIL_PATTERNS = ((128, 1), (512, 4), (2048, 16))
DIL_HEADS, DIL_HD = 4, 64
SPA_HEADS, SPA_HD = 16, 64
IDX_HEADS, IDX_HD = 8, 64
TOPK_MAX = 256
X_HEADS = 4
X_HD = D_MODEL // X_HEADS

LANES = 128
MXU_WIDTH = 256
VMEM_LIMIT_BYTES = 56 * 1024 * 1024

_MX = jnp.bfloat16
_F32 = jnp.float32
_NEG = -1e30


def _dot(a, b):
    return jnp.dot(a, b, preferred_element_type=_F32)


def _dot_nt(a, b):
    return lax.dot_general(a, b, (((1,), (1,)), ((), ())), preferred_element_type=_F32)


def _params(sem, flags=None):
    return pltpu.CompilerParams(dimension_semantics=sem, vmem_limit_bytes=VMEM_LIMIT_BYTES, flags=flags)


def _row_rmsnorm(x, g):
    return x * lax.rsqrt(jnp.mean(x * x, axis=-1, keepdims=True) + EPS) * g


def _seg_sumsq(y, seg_ref):
    y2 = y * y
    hi = y2.astype(_MX)
    lo = (y2 - hi.astype(_F32)).astype(_MX)
    s = seg_ref[...]
    return _dot(hi, s) + _dot(lo, s)


def _swap_lanes(z, d):
    lane = lax.broadcasted_iota(jnp.int32, z.shape, 1)
    fwd = pltpu.roll(z, LANES - d, 1)
    bwd = pltpu.roll(z, d, 1)
    return jnp.where((lane & d) == 0, fwd, bwd)


def _rope_tile(z, d, cos, sin_signed):
    return z * cos + _swap_lanes(z, d) * sin_signed


def _run_pipelined(jobs):
    y_next = jobs[0][0]()
    for i, (_, epilogue) in enumerate(jobs):
        y = y_next
        if i + 1 < len(jobs):
            y_next = jobs[i + 1][0]()
        epilogue(y)


FFN_TM = 512
FFN_TF = 1408
FFN_NK = D_FF // FFN_TF


def _ffn_kernel(x_ref, g_ref, wg_ref, wu_ref, w2_ref, o_ref, h_scr, acc_scr):
    k = pl.program_id(1)

    @pl.when(k == 0)
    def _():
        h_scr[...] = _row_rmsnorm(x_ref[...], g_ref[...]).astype(_MX)
        acc_scr[...] = jnp.zeros_like(acc_scr)

    h = h_scr[...]
    gate = _dot(h, wg_ref[...])
    up = _dot(h, wu_ref[...])
    act = (jax.nn.silu(gate) * up).astype(_MX)
    acc_scr[...] += _dot(act, w2_ref[...])

    @pl.when(k == FFN_NK - 1)
    def _():
        o_ref[...] = x_ref[...] + 0.5 * acc_scr[...]


def _ffn(x, g, w13, w2, layer):
    n = x.shape[0]
    return pl.pallas_call(
        _ffn_kernel,
        grid=(n // FFN_TM, FFN_NK),
        in_specs=[
            pl.BlockSpec((FFN_TM, D_MODEL), lambda i, k: (i, 0)),
            pl.BlockSpec((None, 1, D_MODEL), lambda i, k: (layer, 0, 0)),
            pl.BlockSpec((None, D_MODEL, FFN_TF), lambda i, k: (layer, 0, k)),
            pl.BlockSpec((None, D_MODEL, FFN_TF), lambda i, k: (layer, 0, FFN_NK + k)),
            pl.BlockSpec((None, FFN_TF, D_MODEL), lambda i, k: (layer, k, 0)),
        ],
        out_specs=pl.BlockSpec((FFN_TM, D_MODEL), lambda i, k: (i, 0)),
        out_shape=jax.ShapeDtypeStruct(x.shape, _F32),
        scratch_shapes=[pltpu.VMEM((FFN_TM, D_MODEL), _MX), pltpu.VMEM((FFN_TM, D_MODEL), _F32)],
        compiler_params=_params(("parallel", "arbitrary")),
        name="ffn",
    )(x, g, w13, w13, w2)


PROJ_TM = 256
EVEN_IN_PAD = 2816
DIL_COLS = 2304


def _even_proj_kernel(x_ref, g_ref, win_ref, wq_ref, wk_ref, wv_ref, pin_ref, pq_ref, pk_ref,
                      s64_ref, s128_ref, s256_ref, c64_ref, n64_ref, cm_ref, nm_ref,
                      qm_ref, km_ref, vt_ref, dil_ref):
    h = _row_rmsnorm(x_ref[...], g_ref[...]).astype(_MX)
    c64, n64, cm, nm = c64_ref[...], n64_ref[...], cm_ref[...], nm_ref[...]

    def normed(y, seg_ref, p_ref, c0):
        ss = _seg_sumsq(y, seg_ref)
        return y * lax.rsqrt(ss * p_ref[0:1, c0:c0 + MXU_WIDTH] + EPS) * p_ref[1:2, c0:c0 + MXU_WIDTH]

    tiles = lambda c0: [(t, slice(c0 + LANES * t, c0 + LANES * (t + 1))) for t in range(2)]
    latent = {}
    jobs = []

    def epi_cq(y):
        latent["cq"] = normed(y, s256_ref, pin_ref, 0).astype(_MX)
    jobs.append((lambda: _dot(h, win_ref[:, 0:256]), epi_cq))

    def epi_ckv(y):
        z = normed(y, s128_ref, pin_ref, 256)
        latent["ckv"] = z[:, :LANES].astype(_MX)
        latent["k_rope"] = _rope_tile(z[:, LANES:], MLA_ROPE // 2, cm, nm)
    jobs.append((lambda: _dot(h, win_ref[:, 256:512]), epi_ckv))

    for j in range(DIL_COLS // MXU_WIDTH):
        def epi_dil(y, j=j):
            if j % 3 < 2:
                z = normed(y, s64_ref, pin_ref, 512 + MXU_WIDTH * j)
                for t, cols in tiles(MXU_WIDTH * j):
                    dil_ref[:, cols] = _rope_tile(z[:, LANES * t:LANES * (t + 1)], DIL_HD // 2, c64, n64)
            else:
                dil_ref[:, MXU_WIDTH * j:MXU_WIDTH * (j + 1)] = y
        jobs.append((lambda j=j: _dot(h, win_ref[:, 512 + MXU_WIDTH * j:512 + MXU_WIDTH * (j + 1)]), epi_dil))

    for j in range(MLA_HEADS * LANES // MXU_WIDTH):
        c0 = MXU_WIDTH * j
        def epi_q(y, c0=c0):
            z = normed(y, s64_ref, pq_ref, c0)
            for t, cols in tiles(c0):
                qm_ref[:, cols] = _rope_tile(z[:, LANES * t:LANES * (t + 1)], MLA_ROPE // 2, cm, nm).astype(_MX)
        jobs.append((lambda c0=c0: _dot(latent["cq"], wq_ref[:, c0:c0 + MXU_WIDTH]), epi_q))

        def epi_k(y, c0=c0):
            z = normed(y, s64_ref, pk_ref, c0)
            for t, cols in tiles(c0):
                km_ref[:, cols] = (z[:, LANES * t:LANES * (t + 1)] + latent["k_rope"]).astype(_MX)
        jobs.append((lambda c0=c0: _dot(latent["ckv"], wk_ref[:, c0:c0 + MXU_WIDTH]), epi_k))

    for j in range(MLA_HEADS * MLA_V // MXU_WIDTH):
        c0 = MXU_WIDTH * j
        def epi_v(y, c0=c0):
            for t, cols in tiles(c0):
                vt_ref[cols, :] = y[:, LANES * t:LANES * (t + 1)].T.astype(_MX)
        jobs.append((lambda c0=c0: _dot(latent["ckv"], wv_ref[:, c0:c0 + MXU_WIDTH]), epi_v))
    _run_pipelined(jobs)


def _const_spec(shape):
    nd = len(shape)
    return pl.BlockSpec(shape, lambda i: (0,) * nd)


def _even_proj(x, g, win, wq, wk, wv, pin, pq, pk, segs, tabs):
    n = x.shape[0]
    tm = PROJ_TM
    row = lambda w: pl.BlockSpec((tm, w), lambda i: (i, 0))
    s64, s128, s256 = segs
    c64, n64, cm, nm = tabs
    return pl.pallas_call(
        _even_proj_kernel,
        grid=(n // tm,),
        in_specs=[row(D_MODEL), _const_spec(g.shape), _const_spec(win.shape), _const_spec(wq.shape),
                  _const_spec(wk.shape), _const_spec(wv.shape), _const_spec(pin.shape),
                  _const_spec(pq.shape), _const_spec(pk.shape), _const_spec(s64.shape),
                  _const_spec(s128.shape), _const_spec(s256.shape),
                  row(LANES), row(LANES), row(LANES), row(LANES)],
        out_specs=[row(MLA_HEADS * LANES), row(MLA_HEADS * LANES),
                   pl.BlockSpec((None, MLA_HEADS * MLA_V, tm), lambda i: (i, 0, 0)), row(DIL_COLS)],
        out_shape=[jax.ShapeDtypeStruct((n, MLA_HEADS * LANES), _MX),
                   jax.ShapeDtypeStruct((n, MLA_HEADS * LANES), _MX),
                   jax.ShapeDtypeStruct((n // tm, MLA_HEADS * MLA_V, tm), _MX),
                   jax.ShapeDtypeStruct((n, DIL_COLS), _F32)],
        compiler_params=_params(("parallel",)),
        name="even_proj",
    )(x, g, win, wq, wk, wv, pin, pq, pk, s64, s128, s256, c64, n64, cm, nm)


MLA_TQ = 256
MLA_TK = 256


def _mla_attn_kernel(q_ref, k_ref, vt_ref, o_ref, s_scr, m_scr, l_scr, acc_scr):
    qi = pl.program_id(1)
    scale = (MLA_NOPE + MLA_ROPE) ** -0.5
    m_scr[...] = jnp.full_like(m_scr, _NEG)
    l_scr[...] = jnp.zeros_like(l_scr)
    acc_scr[...] = jnp.zeros_like(acc_scr)
    kpos = lax.broadcasted_iota(jnp.int32, (MLA_TK, MLA_TQ), 0)
    qpos = qi * MLA_TQ + lax.broadcasted_iota(jnp.int32, (MLA_TK, MLA_TQ), 1)

    def score_pass(c, carry):
        k0 = pl.multiple_of(c * MLA_TK, MLA_TK)
        visible = (k0 + kpos) <= qpos
        for hd in range(MLA_HEADS):
            cols = slice(LANES * hd, LANES * (hd + 1))
            st = _dot_nt(k_ref[pl.ds(k0, MLA_TK), cols], q_ref[:, cols])
            st = jnp.where(visible, st * scale, _NEG)
            s_scr[c, hd] = st
            m_scr[hd] = jnp.maximum(m_scr[hd], jnp.max(st, axis=0, keepdims=True))
        return carry

    def value_pass(c, carry):
        for hd in range(MLA_HEADS):
            p = jnp.exp(s_scr[c, hd] - m_scr[hd])
            l_scr[hd] += jnp.sum(p, axis=0, keepdims=True)
            vt = vt_ref[c, MLA_V * hd:MLA_V * (hd + 1), :]
            acc_scr[hd] += _dot(vt, p.astype(_MX))
        return carry

    lax.fori_loop(0, qi + 1, score_pass, 0)
    lax.fori_loop(0, qi + 1, value_pass, 0)
    ot = jnp.concatenate([acc_scr[hd] / l_scr[hd] for hd in range(MLA_HEADS)], axis=0)
    o_ref[...] = ot.T.astype(o_ref.dtype)


def _mla_attn(qm, km, vt, batch, seq):
    n = qm.shape[0]
    nq = seq // MLA_TQ
    wv = MLA_HEADS * MLA_V
    return pl.pallas_call(
        _mla_attn_kernel,
        grid=(batch, nq),
        in_specs=[
            pl.BlockSpec((MLA_TQ, MLA_HEADS * LANES), lambda b, i: (b * nq + i, 0)),
            pl.BlockSpec((seq, MLA_HEADS * LANES), lambda b, i: (b, 0)),
            pl.BlockSpec((seq // MLA_TK, wv, MLA_TK), lambda b, i: (b, 0, 0)),
        ],
        out_specs=pl.BlockSpec((MLA_TQ, wv), lambda b, i: (b * nq + i, 0)),
        out_shape=jax.ShapeDtypeStruct((n, wv), _MX),
        scratch_shapes=[pltpu.VMEM((seq // MLA_TK, MLA_HEADS, MLA_TK, MLA_TQ), _F32),
                        pltpu.VMEM((MLA_HEADS, 1, MLA_TQ), _F32), pltpu.VMEM((MLA_HEADS, 1, MLA_TQ), _F32),
                        pltpu.VMEM((MLA_HEADS, MLA_V, MLA_TQ), _F32)],
        compiler_params=_params(("parallel", "arbitrary")),
        name="mla_attn",
    )(qm, km, vt)


DIL_BLK = 128
DIL_UNROLL = 8


def _dil_attn_kernel(*refs, seq):
    n_grp = len(DIL_PATTERNS)
    qkv = [refs[3 * g:3 * g + 3] for g in range(n_grp)]
    out_ref = refs[3 * n_grp]
    o_slabs = refs[3 * n_grp + 1:3 * n_grp + 1 + n_grp]
    l_slabs = refs[3 * n_grp + 1 + n_grp:]
    scale = DIL_HD ** -0.5
    kidx = lax.broadcasted_iota(jnp.int32, (DIL_BLK, DIL_BLK), 0)
    qidx = lax.broadcasted_iota(jnp.int32, (DIL_BLK, DIL_BLK), 1)
    lane = lax.broadcasted_iota(jnp.int32, (DIL_BLK, LANES), 1)
    dim = lax.broadcasted_iota(jnp.int32, (LANES, DIL_BLK), 0)
    cur_ok = kidx <= qidx

    for g, (window, d) in enumerate(DIL_PATTERNS):
        q_ref, k_ref, v_ref = qkv[g]
        nb = seq // d // DIL_BLK

        def units(it, carry, d=d, nb=nb, q_ref=q_ref, k_ref=k_ref, v_ref=v_ref, g=g):
            staged = []
            for u in range(DIL_UNROLL):
                idx = it * DIL_UNROLL + u
                r = idx // nb
                n = idx % nb
                rows = pl.ds(r + d * DIL_BLK * n, DIL_BLK, stride=d)
                prev = pl.ds(r + d * DIL_BLK * jnp.maximum(n - 1, 0), DIL_BLK, stride=d)
                prev_ok = kidx >= qidx + jnp.where(n > 0, 0, DIL_BLK)
                q2 = q_ref[rows, :] * scale
                kc = k_ref[rows, :].astype(_MX)
                kp = k_ref[prev, :].astype(_MX)
                halves = []
                for half in range(2):
                    mine = (lane < DIL_HD) if half == 0 else (lane >= DIL_HD)
                    qh = jnp.where(mine, q2, 0.0).astype(_MX)
                    halves.append((_dot_nt(kc, qh), _dot_nt(kp, qh)))
                staged.append((rows, prev, prev_ok, halves))
            for rows, prev, prev_ok, halves in staged:
                vct = v_ref[rows, :].T.astype(_MX)
                vpt = v_ref[prev, :].T.astype(_MX)
                ots, lses = [], []
                for sc, sp in halves:
                    sc = jnp.where(cur_ok, sc, _NEG)
                    sp = jnp.where(prev_ok, sp, _NEG)
                    m = jnp.maximum(jnp.max(sc, axis=0, keepdims=True), jnp.max(sp, axis=0, keepdims=True))
                    pc = jnp.exp(sc - m)
                    pp = jnp.exp(sp - m)
                    l = jnp.sum(pc, axis=0, keepdims=True) + jnp.sum(pp, axis=0, keepdims=True)
                    ots.append((_dot(vct, pc.astype(_MX)) + _dot(vpt, pp.astype(_MX))) / l)
                    lses.append(m + jnp.log(l))
                ot = jnp.where(dim < DIL_HD, ots[0], ots[1])
                lt = jnp.where(dim < DIL_HD, lses[0], lses[1])
                o_slabs[g][rows, :] = ot.T
                l_slabs[g][rows, :] = lt.T
            return carry

        lax.fori_loop(0, d * nb // DIL_UNROLL, units, 0)

    ls = [l_slabs[g][...] for g in range(n_grp)]
    m = functools.reduce(jnp.maximum, ls)
    es = [jnp.exp(l - m) for l in ls]
    num = sum(e * o_slabs[g][...] for g, e in enumerate(es))
    out_ref[...] = (num / sum(es)).astype(out_ref.dtype)


def _dil_attn(dil, batch, seq):
    n = dil.shape[0]
    pairs = DIL_HEADS // 2
    spec = lambda g, part: pl.BlockSpec((seq, LANES), lambda b, p: (b, (3 * g + part) * pairs + p))
    in_specs = [spec(g, part) for g in range(len(DIL_PATTERNS)) for part in range(3)]
    return pl.pallas_call(
        functools.partial(_dil_attn_kernel, seq=seq),
        grid=(batch, pairs),
        in_specs=in_specs,
        out_specs=pl.BlockSpec((seq, LANES), lambda b, p: (b, p)),
        out_shape=jax.ShapeDtypeStruct((n, DIL_HEADS * DIL_HD), _MX),
        scratch_shapes=[pltpu.VMEM((seq, LANES), _F32)] * (2 * len(DIL_PATTERNS)),
        compiler_params=_params(("parallel", "parallel")),
        name="dil_attn",
    )(*([dil] * len(in_specs)))


OUT_TM = 512


def _even_out_kernel(x_ref, mla_ref, dil_ref, w_ref, out_ref):
    nm = MLA_HEADS * MLA_V
    mix = _dot(mla_ref[...], w_ref[0:nm, :]) + _dot(dil_ref[...], w_ref[nm:, :])
    out_ref[...] = x_ref[...] + mix


def _even_out(x, mla, dil, w):
    n = x.shape[0]
    tm = OUT_TM
    row = lambda wd: pl.BlockSpec((tm, wd), lambda i: (i, 0))
    return pl.pallas_call(
        _even_out_kernel,
        grid=(n // tm,),
        in_specs=[row(D_MODEL), row(MLA_HEADS * MLA_V), row(DIL_HEADS * DIL_HD), _const_spec(w.shape)],
        out_specs=row(D_MODEL),
        out_shape=jax.ShapeDtypeStruct(x.shape, _F32),
        compiler_params=_params(("parallel",)),
        name="even_out",
    )(x, mla, dil, w)


ODD_IN_PAD = 2048


def _odd_proj_kernel(x_ref, g_ref, win_ref, pin_ref, s64_ref, c64_ref, n64_ref,
                     q_ref, kk_ref, vt_ref, qi_ref, ki_ref, wit_ref):
    h = _row_rmsnorm(x_ref[...], g_ref[...]).astype(_MX)
    c64, n64 = c64_ref[...], n64_ref[...]

    def normed(y, c0):
        ss = _seg_sumsq(y, s64_ref)
        return y * lax.rsqrt(ss * pin_ref[0:1, c0:c0 + MXU_WIDTH] + EPS) * pin_ref[1:2, c0:c0 + MXU_WIDTH]

    def rope64(t):
        return _rope_tile(t, SPA_HD // 2, c64, n64)

    nq = SPA_HEADS * SPA_HD
    c_qi = nq + MXU_WIDTH
    c_ki = c_qi + IDX_HEADS * IDX_HD
    product = lambda c0: (lambda: _dot(h, win_ref[:, c0:c0 + MXU_WIDTH]))
    jobs = []
    for j in range(nq // MXU_WIDTH):
        def epi_q(y, c0=MXU_WIDTH * j):
            z = normed(y, c0)
            for t in range(2):
                q_ref[:, c0 + LANES * t:c0 + LANES * (t + 1)] = rope64(z[:, LANES * t:LANES * (t + 1)]).astype(_MX)
        jobs.append((product(MXU_WIDTH * j), epi_q))

    def epi_kv(y):
        z = normed(y, nq)
        kk_ref[...] = rope64(z[:, :LANES]).astype(_MX)
        vt_ref[...] = y[:, LANES:].T.astype(_MX)
    jobs.append((product(nq), epi_kv))

    for j in range(IDX_HEADS * IDX_HD // MXU_WIDTH):
        def epi_qi(y, o0=MXU_WIDTH * j):
            for t in range(2):
                qi_ref[:, o0 + LANES * t:o0 + LANES * (t + 1)] = rope64(y[:, LANES * t:LANES * (t + 1)]).astype(_MX)
        jobs.append((product(c_qi + MXU_WIDTH * j), epi_qi))

    def epi_ki(y):
        ki_ref[...] = rope64(y[:, :LANES]).astype(_MX)
        wi = (y[:, LANES:] * (IDX_HEADS ** -0.5)) * (IDX_HD ** -0.5)
        wit_ref[...] = wi.T[0:IDX_HEADS, :]
    jobs.append((product(c_ki), epi_ki))
    _run_pipelined(jobs)


def _odd_proj(x, g, win, pin, s64, tabs):
    n = x.shape[0]
    tm = PROJ_TM
    row = lambda w: pl.BlockSpec((tm, w), lambda i: (i, 0))
    c64, n64 = tabs
    return pl.pallas_call(
        _odd_proj_kernel,
        grid=(n // tm,),
        in_specs=[row(D_MODEL), _const_spec(g.shape), _const_spec(win.shape), _const_spec(pin.shape),
                  _const_spec(s64.shape), row(LANES), row(LANES)],
        out_specs=[row(SPA_HEADS * SPA_HD), row(LANES),
                   pl.BlockSpec((None, LANES, tm), lambda i: (i, 0, 0)),
                   row(IDX_HEADS * IDX_HD), row(LANES),
                   pl.BlockSpec((IDX_HEADS, tm), lambda i: (0, i))],
        out_shape=[jax.ShapeDtypeStruct((n, SPA_HEADS * SPA_HD), _MX),
                   jax.ShapeDtypeStruct((n, LANES), _MX),
                   jax.ShapeDtypeStruct((n // tm, LANES, tm), _MX),
                   jax.ShapeDtypeStruct((n, IDX_HEADS * IDX_HD), _MX),
                   jax.ShapeDtypeStruct((n, LANES), _MX),
                   jax.ShapeDtypeStruct((IDX_HEADS, n), _F32)],
        compiler_params=_params(("parallel",)),
        name="odd_proj",
    )(x, g, win, pin, s64, c64, n64)


SPA_TQ = 128
SPA_TK = 256
_INT_MIN = -2 ** 31
assert SPA_TK == PROJ_TM and MLA_TK == PROJ_TM


def _sparse_attn_kernel(q_ref, kk_ref, vt_ref, qi_ref, ki_ref, wit_ref, o_ref,
                        key_scr, qs_scr, s_scr, m_scr, acc_scr, *, n_keep):
    qb = pl.program_id(1)
    n_chunks = qb // (SPA_TK // SPA_TQ) + 1
    lane = lax.broadcasted_iota(jnp.int32, (SPA_TQ, LANES), 1)
    left = lane < SPA_HD

    def stack_heads(ref, n_pairs):
        parts = []
        for pr in range(n_pairs):
            t = ref[:, LANES * pr:LANES * (pr + 1)]
            parts.append(jnp.where(left, t, jnp.zeros_like(t)))
            parts.append(jnp.where(left, jnp.zeros_like(t), t))
        return jnp.concatenate(parts, axis=0)

    qi_stack = stack_heads(qi_ref, IDX_HEADS // 2)
    wit = wit_ref[...]
    krow = lax.broadcasted_iota(jnp.int32, (SPA_TK, SPA_TQ), 0)
    qpos = qb * SPA_TQ + lax.broadcasted_iota(jnp.int32, (SPA_TK, SPA_TQ), 1)

    def score_chunk(c, carry):
        k0 = pl.multiple_of(c * SPA_TK, SPA_TK)
        logits = _dot_nt(ki_ref[pl.ds(k0, SPA_TK), :], qi_stack)
        sc = jnp.zeros((SPA_TK, SPA_TQ), _F32)
        for hd in range(IDX_HEADS):
            sc = sc + wit[hd:hd + 1, :] * jnp.maximum(logits[:, SPA_TQ * hd:SPA_TQ * (hd + 1)], 0.0)
        sc = jnp.where((k0 + krow) <= qpos, sc, -jnp.inf)
        bits = pltpu.bitcast(sc, jnp.int32)
        key_scr[c] = bits ^ ((bits >> 31) & jnp.int32(0x7FFFFFFF))
        return carry

    lax.fori_loop(0, n_chunks, score_chunk, 0)

    def count(pred):
        def body(c, acc):
            hit = pred(key_scr[c], c * SPA_TK)
            return acc + jnp.sum(hit.astype(jnp.int32).reshape(SPA_TK // 8, 8, SPA_TQ), axis=0)
        acc = lax.fori_loop(0, n_chunks, body, jnp.zeros((8, SPA_TQ), jnp.int32))
        return jnp.sum(acc, axis=0, keepdims=True)

    def refine(t, cand):
        return jnp.where(count(lambda k, k0: k >= cand) >= n_keep, cand, t)

    t = jnp.full((1, SPA_TQ), _INT_MIN, jnp.int32)
    t = refine(t, jnp.zeros((1, SPA_TQ), jnp.int32))
    t = lax.fori_loop(0, 31, lambda i, t: refine(t, t | (jnp.int32(1) << (30 - i))), t)
    n_gt = count(lambda k, k0: k > t)
    n_ge = count(lambda k, k0: k >= t)
    need = n_keep - n_gt
    n_bits = (key_scr.shape[0] * SPA_TK).bit_length()

    def search_last():
        def refine_pos(j, cand):
            below = count(lambda k, k0: (k == t) & ((k0 + krow) < cand))
            return jnp.where(below < need, cand, j)
        return lax.fori_loop(0, n_bits, lambda i, j: refine_pos(j, j | (jnp.int32(1) << (n_bits - 1 - i))),
                             jnp.zeros((1, SPA_TQ), jnp.int32))

    keep_all_ties = lambda: jnp.full((1, SPA_TQ), 2 ** n_bits, jnp.int32)
    last = lax.cond(jnp.max(n_ge) > n_keep, search_last, keep_all_ties)

    qs_scr[...] = stack_heads(q_ref, SPA_HEADS // 2) * (SPA_HD ** -0.5)
    m_scr[...] = jnp.full_like(m_scr, _NEG)
    acc_scr[...] = jnp.zeros_like(acc_scr)
    vrow = lax.broadcasted_iota(jnp.int32, (LANES, SPA_TK), 0)
    pair_w = 2 * SPA_TQ
    pairs = [slice(pair_w * pr, pair_w * (pr + 1)) for pr in range(SPA_HEADS // 2)]

    def score_pass(c, carry):
        k0 = pl.multiple_of(c * SPA_TK, SPA_TK)
        keys = key_scr[c]
        kpos = k0 + krow
        chosen = ((keys > t) | ((keys == t) & (kpos <= last))) & (kpos <= qpos)
        bias = jnp.where(chosen, 0.0, _NEG)
        bias2 = jnp.concatenate([bias, bias], axis=1)
        kk = kk_ref[pl.ds(k0, SPA_TK), :]
        for cols in pairs:
            st = _dot_nt(kk, qs_scr[cols, :]) + bias2
            s_scr[c, :, cols] = st
            m_scr[:, cols] = jnp.maximum(m_scr[:, cols], jnp.max(st, axis=0, keepdims=True))
        return carry

    def value_pass(c, carry):
        vt = vt_ref[c]
        v1t = jnp.where(vrow < SPA_HD, vt, jnp.ones_like(vt))
        for cols in pairs:
            p = jnp.exp(s_scr[c, :, cols] - m_scr[:, cols])
            acc_scr[:, cols] += _dot(v1t, p.astype(_MX))
        return carry

    lax.fori_loop(0, n_chunks, score_pass, 0)
    lax.fori_loop(0, n_chunks, value_pass, 0)
    for pr in range(SPA_HEADS // 2):
        a = acc_scr[:, pair_w * pr:pair_w * pr + SPA_TQ]
        b = acc_scr[:, pair_w * pr + SPA_TQ:pair_w * (pr + 1)]
        tile_t = jnp.concatenate([a[:SPA_HD] / a[SPA_HD:SPA_HD + 1], b[:SPA_HD] / b[SPA_HD:SPA_HD + 1]], axis=0)
        o_ref[:, LANES * pr:LANES * (pr + 1)] = tile_t.T.astype(o_ref.dtype)


def _sparse_attn(q, kk, vt, qi, ki, wit, batch, seq):
    n = q.shape[0]
    nq = seq // SPA_TQ
    n_keep = min(TOPK_MAX, seq // 4)
    return pl.pallas_call(
        functools.partial(_sparse_attn_kernel, n_keep=n_keep),
        grid=(batch, nq),
        in_specs=[
            pl.BlockSpec((SPA_TQ, SPA_HEADS * SPA_HD), lambda b, i: (b * nq + i, 0)),
            pl.BlockSpec((seq, LANES), lambda b, i: (b, 0)),
            pl.BlockSpec((seq // SPA_TK, LANES, SPA_TK), lambda b, i: (b, 0, 0)),
            pl.BlockSpec((SPA_TQ, IDX_HEADS * IDX_HD), lambda b, i: (b * nq + i, 0)),
            pl.BlockSpec((seq, LANES), lambda b, i: (b, 0)),
            pl.BlockSpec((IDX_HEADS, SPA_TQ), lambda b, i: (0, b * nq + i)),
        ],
        out_specs=pl.BlockSpec((SPA_TQ, SPA_HEADS * SPA_HD), lambda b, i: (b * nq + i, 0)),
        out_shape=jax.ShapeDtypeStruct((n, SPA_HEADS * SPA_HD), _MX),
        scratch_shapes=[pltpu.VMEM((seq // SPA_TK, SPA_TK, SPA_TQ), jnp.int32),
                        pltpu.VMEM((SPA_HEADS * SPA_TQ, LANES), _MX),
                        pltpu.VMEM((seq // SPA_TK, SPA_TK, SPA_HEADS * SPA_TQ), _F32),
                        pltpu.VMEM((1, SPA_HEADS * SPA_TQ), _F32),
                        pltpu.VMEM((LANES, SPA_HEADS * SPA_TQ), _F32)],
        compiler_params=_params(("parallel", "arbitrary")),
        name="sparse_attn",
    )(q, kk, vt, qi, ki, wit)


def _proj_residual_kernel(x_ref, a_ref, w_ref, out_ref):
    out_ref[...] = x_ref[...] + _dot(a_ref[...], w_ref[...])


def _proj_residual(x, a, w):
    n = x.shape[0]
    tm = OUT_TM
    return pl.pallas_call(
        _proj_residual_kernel,
        grid=(n // tm,),
        in_specs=[pl.BlockSpec((tm, D_MODEL), lambda i: (i, 0)),
                  pl.BlockSpec((tm, a.shape[1]), lambda i: (i, 0)), _const_spec(w.shape)],
        out_specs=pl.BlockSpec((tm, D_MODEL), lambda i: (i, 0)),
        out_shape=jax.ShapeDtypeStruct(x.shape, _F32),
        compiler_params=_params(("parallel",)),
        name="odd_out",
    )(x, a, w)


def _mem_kv_kernel(mem_ref, g_ref, w_ref, kg_ref, k_ref, v_ref):
    h = _row_rmsnorm(mem_ref[...], g_ref[...]).astype(_MX)
    kg = kg_ref[...]
    for hd in range(X_HEADS):
        cols = slice(X_HD * hd, X_HD * (hd + 1))
        y = _dot(h, w_ref[:, cols])
        k_ref[:, cols] = _row_rmsnorm(y, kg).astype(_MX)
    for hd in range(X_HEADS):
        cols = slice(X_HD * hd, X_HD * (hd + 1))
        v_ref[:, cols] = _dot(h, w_ref[:, D_MODEL + X_HD * hd:D_MODEL + X_HD * (hd + 1)]).astype(_MX)


def _mem_kv(mem, g, wkv, kg):
    n, m = mem.shape[0], 256
    row = pl.BlockSpec((m, D_MODEL), lambda i: (i, 0))
    return pl.pallas_call(
        _mem_kv_kernel,
        grid=(n // m,),
        in_specs=[row, _const_spec(g.shape), _const_spec(wkv.shape), _const_spec(kg.shape)],
        out_specs=[row, row],
        out_shape=[jax.ShapeDtypeStruct((n, D_MODEL), _MX)] * 2,
        compiler_params=_params(("parallel",)),
        name="mem_kv",
    )(mem, g, wkv, kg)


XATT_TM = 512


def _xattn_kernel(x_ref, g_ref, wq_ref, qg_ref, k_ref, v_ref, wo_ref, out_ref, o_scr):
    x = x_ref[...]
    h = _row_rmsnorm(x, g_ref[...]).astype(_MX)
    qg = qg_ref[...]
    scale = X_HD ** -0.5
    heads = [slice(X_HD * hd, X_HD * (hd + 1)) for hd in range(X_HEADS)]
    qs = [_dot(h, wq_ref[:, cols]) for cols in heads]
    qs = [_row_rmsnorm(q, qg).astype(_MX) for q in qs]
    ss = [_dot_nt(q, k_ref[:, cols]) * scale for q, cols in zip(qs, heads)]
    for s, cols in zip(ss, heads):
        p = jnp.exp(s - jnp.max(s, axis=-1, keepdims=True))
        p = p / jnp.sum(p, axis=-1, keepdims=True)
        o_scr[:, cols] = _dot(p.astype(_MX), v_ref[:, cols]).astype(_MX)
    out_ref[...] = x + _dot(o_scr[...], wo_ref[...])


def _xattn(x, g, wq, qg, k, v, wo, batch, seq, mem_len):
    n = x.shape[0]
    tm = XATT_TM
    nt = seq // tm
    row = pl.BlockSpec((tm, D_MODEL), lambda b, i: (b * nt + i, 0))
    const = lambda a: pl.BlockSpec(a.shape, lambda b, i: (0,) * a.ndim)
    memspec = pl.BlockSpec((mem_len, D_MODEL), lambda b, i: (b, 0))
    return pl.pallas_call(
        _xattn_kernel,
        grid=(batch, nt),
        in_specs=[row, const(g), const(wq), const(qg), memspec, memspec, const(wo)],
        out_specs=row,
        out_shape=jax.ShapeDtypeStruct(x.shape, _F32),
        scratch_shapes=[pltpu.VMEM((tm, D_MODEL), _MX)],
        compiler_params=_params(("parallel", "parallel")),
        name="xattn",
    )(x, g, wq, qg, k, v, wo)


def _seg_matrix(seg):
    idx = np.arange(MXU_WIDTH) // seg
    return jnp.asarray(idx[:, None] == idx[None, :], _MX)


def _rope_tables(positions, dim):
    inv = jnp.exp(-math.log(ROPE_THETA) * jnp.arange(0, dim, 2, dtype=_F32) / dim)
    ang = positions.astype(_F32).reshape(-1)[:, None] * inv
    return jnp.cos(ang), jnp.sin(ang)


def _tables(positions):
    cos64, sin64 = _rope_tables(positions, DIL_HD)
    cos32, sin32 = _rope_tables(positions, MLA_ROPE)
    n = cos64.shape[0]
    c64 = jnp.tile(cos64, (1, 4))
    n64 = jnp.tile(jnp.concatenate([-sin64, sin64], axis=1), (1, 2))
    ones, zeros = jnp.ones((n, MLA_NOPE), _F32), jnp.zeros((n, MLA_NOPE), _F32)
    pad1, pad0 = jnp.ones((n, 32), _F32), jnp.zeros((n, 32), _F32)
    cm = jnp.concatenate([ones, cos32, cos32, pad1], axis=1)
    nm = jnp.concatenate([zeros, -sin32, sin32, pad0], axis=1)
    return c64, n64, cm, nm


def _even_weights(w_in, q_lora_g, kv_lora_g, w_uq, w_ukv, q_g, k_g, dq_g, dk_g, w_out):
    o1, o2, o3 = MLA_Q_LORA, MLA_Q_LORA + MLA_KV_LORA, MLA_Q_LORA + MLA_KV_LORA + MLA_ROPE
    d = w_in.shape[0]
    z = lambda w: jnp.zeros((d, w), _F32)
    win = jnp.concatenate([w_in[:, :o2], z(64), w_in[:, o2:o3], z(32), w_in[:, o3:]], axis=1).astype(_MX)
    ones = lambda w: jnp.ones((w,), _F32)
    zeros = lambda w: jnp.zeros((w,), _F32)
    grp_gain = jnp.concatenate([jnp.tile(dq_g, DIL_HEADS), jnp.tile(dk_g, DIL_HEADS), ones(256)])
    pin = jnp.stack([
        jnp.concatenate([ones(256) / 256, ones(128) / 128, ones(128) / MLA_ROPE, ones(DIL_COLS) / DIL_HD]),
        jnp.concatenate([q_lora_g, kv_lora_g, zeros(64), k_g[MLA_NOPE:], zeros(32), jnp.tile(grp_gain, 3)]),
    ])
    wq = jnp.pad(w_uq.reshape(MLA_Q_LORA, MLA_HEADS, MLA_NOPE + MLA_ROPE), ((0, 0), (0, 0), (0, 32)))
    wq = wq.reshape(MLA_Q_LORA, MLA_HEADS * LANES).astype(_MX)
    pq = jnp.stack([
        jnp.tile(jnp.concatenate([ones(64) / MLA_NOPE, ones(64) / MLA_ROPE]), MLA_HEADS),
        jnp.tile(jnp.concatenate([q_g, zeros(32)]), MLA_HEADS),
    ])
    ukv = w_ukv.reshape(MLA_KV_LORA, MLA_HEADS, MLA_NOPE + MLA_V)
    wk = jnp.pad(ukv[:, :, :MLA_NOPE], ((0, 0), (0, 0), (0, 64))).reshape(MLA_KV_LORA, MLA_HEADS * LANES).astype(_MX)
    wv = ukv[:, :, MLA_NOPE:].reshape(MLA_KV_LORA, MLA_HEADS * MLA_V).astype(_MX)
    pk = jnp.stack([
        jnp.tile(ones(LANES) / MLA_NOPE, MLA_HEADS),
        jnp.tile(jnp.concatenate([k_g[:MLA_NOPE], zeros(64)]), MLA_HEADS),
    ])
    return win, wq, wk, wv, pin, pq, pk, w_out.astype(_MX)


def _odd_weights(w_in, q_g, k_g, w_out):
    o1 = SPA_HEADS * SPA_HD
    o2, o3 = o1 + SPA_HD, o1 + 2 * SPA_HD
    o4 = o3 + IDX_HEADS * IDX_HD
    o5 = o4 + IDX_HD
    d = w_in.shape[0]
    k, v, ki = w_in[:, o1:o2], w_in[:, o2:o3], w_in[:, o4:o5]
    win = jnp.concatenate([w_in[:, :o1], k, k, v, v, w_in[:, o3:o4], ki, ki, w_in[:, o5:],
                           jnp.zeros((d, LANES - IDX_HEADS), _F32)], axis=1).astype(_MX)
    ones = jnp.ones((ODD_IN_PAD,), _F32)
    gain = jnp.concatenate([jnp.tile(q_g, SPA_HEADS), k_g, k_g, jnp.ones((ODD_IN_PAD - o1 - 2 * SPA_HD,), _F32)])
    pin = jnp.stack([ones / SPA_HD, gain])
    return win, pin, w_out.astype(_MX)


def kernel(x, mem, positions, ffn1_norm, ffn1_w13, ffn1_w2, mix_norm, xattn_norm, mem_norm,
           xattn_wq, xattn_wkv, xattn_q_gain, xattn_k_gain, xattn_wo, ffn2_norm, ffn2_w13, ffn2_w2,
           even_w_in, mla_q_lora_norm, mla_kv_lora_norm, mla_w_uq, mla_w_ukv, mla_q_gain, mla_k_gain,
           dil_q_gain, dil_k_gain, even_w_out, odd_w_in, sparse_q_gain, sparse_k_gain, odd_w_out):
    batch, seq, d = x.shape
    mem_len = mem.shape[1]
    n = batch * seq
    c64, n64, cm, nm = _tables(positions)
    s64, s128, s256 = _seg_matrix(64), _seg_matrix(128), _seg_matrix(256)
    bf = lambda a: a.astype(_MX)
    row3 = lambda a: a.reshape(a.shape[0], 1, a.shape[1])
    f1w13, f1w2, f2w13, f2w2 = bf(ffn1_w13), bf(ffn1_w2), bf(ffn2_w13), bf(ffn2_w2)
    f1g, f2g = row3(ffn1_norm), row3(ffn2_norm)
    xwq, xwkv, xwo = bf(xattn_wq), bf(xattn_wkv), bf(xattn_wo)

    xs = x.reshape(n, d)
    mems = mem.reshape(batch * mem_len, d)
    for i in range(DEPTH):
        j = i // 2
        xs = _ffn(xs, f1g, f1w13, f1w2, i)
        g_mix = mix_norm[i][None, :]
        if i % 2 == 0:
            win, wq, wk, wv, pin, pq, pk, wout = _even_weights(
                even_w_in[j], mla_q_lora_norm[j], mla_kv_lora_norm[j], mla_w_uq[j], mla_w_ukv[j],
                mla_q_gain[j], mla_k_gain[j], dil_q_gain[j], dil_k_gain[j], even_w_out[j])
            qm, km, vt, dil = _even_proj(xs, g_mix, win, wq, wk, wv, pin, pq, pk,
                                         (s64, s128, s256), (c64, n64, cm, nm))
            mla = _mla_attn(qm, km, vt, batch, seq)
            xs = _even_out(xs, mla, _dil_attn(dil, batch, seq), wout)
        else:
            win, pin, wout = _odd_weights(odd_w_in[j], sparse_q_gain[j], sparse_k_gain[j], odd_w_out[j])
            q, kk, vt, qi, ki, wit = _odd_proj(xs, g_mix, win, pin, s64, (c64, n64))
            att = _sparse_attn(q, kk, vt, qi, ki, wit, batch, seq)
            xs = _proj_residual(xs, att, wout)
        mk, mv = _mem_kv(mems, mem_norm[i][None, :], xwkv[i], xattn_k_gain[i][None, :])
        xs = _xattn(xs, xattn_norm[i][None, :], xwq[i], xattn_q_gain[i][None, :], mk, mv, xwo[i],
                    batch, seq, mem_len)
        xs = _ffn(xs, f2g, f2w13, f2w2, i)
    return xs.reshape(batch, seq, d)
```

```python
import functools
import math

import jax
import jax.numpy as jnp
import numpy as np
from jax import lax
from jax.experimental import pallas as pl
from jax.experimental.pallas import tpu as pltpu

D_MODEL = 1024
DEPTH = 4
EPS = 1e-6
D_FF = 2816
ROPE_THETA = 10000.0
MLA_HEADS, MLA_NOPE, MLA_ROPE, MLA_V = 8, 64, 32, 64
MLA_Q_LORA, MLA_KV_LORA = 256, 128
DIL_PATTERNS = ((128, 1), (512, 4), (2048, 16))
DIL_HEADS, DIL_HD = 4, 64
SPA_HEADS, SPA_HD = 16, 64
IDX_HEADS, IDX_HD = 8, 64
TOPK_MAX = 256
X_HEADS = 4
X_HD = D_MODEL // X_HEADS

LANES = 128
MXU_WIDTH = 256
VMEM_LIMIT_BYTES = 56 * 1024 * 1024

_MX = jnp.bfloat16
_F32 = jnp.float32
_NEG = -1e30


def _dot(a, b):
    return jnp.dot(a, b, preferred_element_type=_F32)


def _dot_nt(a, b):
    return lax.dot_general(a, b, (((1,), (1,)), ((), ())), preferred_element_type=_F32)


def _params(sem, flags=None):
    return pltpu.CompilerParams(dimension_semantics=sem, vmem_limit_bytes=VMEM_LIMIT_BYTES, flags=flags)


def _row_rmsnorm(x, g):
    return x * lax.rsqrt(jnp.mean(x * x, axis=-1, keepdims=True) + EPS) * g


def _seg_sumsq(y, seg_ref):
    y2 = y * y
    hi = y2.astype(_MX)
    lo = (y2 - hi.astype(_F32)).astype(_MX)
    s = seg_ref[...]
    return _dot(hi, s) + _dot(lo, s)


def _swap_lanes(z, d):
    lane = lax.broadcasted_iota(jnp.int32, z.shape, 1)
    fwd = pltpu.roll(z, LANES - d, 1)
    bwd = pltpu.roll(z, d, 1)
    return jnp.where((lane & d) == 0, fwd, bwd)


def _rope_tile(z, d, cos, sin_signed):
    return z * cos + _swap_lanes(z, d) * sin_signed


def _run_pipelined(jobs):
    y_next = jobs[0][0]()
    for i, (_, epilogue) in enumerate(jobs):
        y = y_next
        if i + 1 < len(jobs):
            y_next = jobs[i + 1][0]()
        epilogue(y)


FFN_TM = 512
FFN_CHUNKS = (768, 768, 768, 512)
assert sum(FFN_CHUNKS) == D_FF and all(c % MXU_WIDTH == 0 for c in FFN_CHUNKS)


def _ffn_kernel(x_ref, g_ref, w13_ref, w2_ref, o_ref, act_scr):
    x = x_ref[...]
    h = _row_rmsnorm(x, g_ref[...]).astype(_MX)
    c0 = 0
    for width in FFN_CHUNKS:
        gate = _dot(h, w13_ref[:, c0:c0 + width])
        up = _dot(h, w13_ref[:, D_FF + c0:D_FF + c0 + width])
        act_scr[:, c0:c0 + width] = (jax.nn.silu(gate) * up).astype(_MX)
        c0 += width
    o_ref[...] = x + 0.5 * _dot(act_scr[...], w2_ref[...])


def _ffn(x, g, w13, w2, layer):
    n = x.shape[0]
    resident = pl.Buffered(1)
    return pl.pallas_call(
        _ffn_kernel,
        grid=(n // FFN_TM,),
        in_specs=[
            pl.BlockSpec((FFN_TM, D_MODEL), lambda i: (i, 0)),
            pl.BlockSpec((None, 1, D_MODEL), lambda i: (layer, 0, 0)),
            pl.BlockSpec((None, D_MODEL, 2 * D_FF), lambda i: (layer, 0, 0), pipeline_mode=resident),
            pl.BlockSpec((None, D_FF, D_MODEL), lambda i: (layer, 0, 0), pipeline_mode=resident),
        ],
        out_specs=pl.BlockSpec((FFN_TM, D_MODEL), lambda i: (i, 0)),
        out_shape=jax.ShapeDtypeStruct(x.shape, _F32),
        scratch_shapes=[pltpu.VMEM((FFN_TM, D_FF), _MX)],
        compiler_params=_params(("parallel",)),
        name="ffn",
    )(x, g, w13, w2)


PROJ_TM = 256
EVEN_IN_PAD = 2816
DIL_COLS = 2304


def _even_proj_kernel(x_ref, g_ref, win_ref, wq_ref, wk_ref, wv_ref, pin_ref, pq_ref, pk_ref,
                      s64_ref, s128_ref, s256_ref, c64_ref, n64_ref, cm_ref, nm_ref,
                      qm_ref, km_ref, vt_ref, dil_ref):
    h = _row_rmsnorm(x_ref[...], g_ref[...]).astype(_MX)
    c64, n64, cm, nm = c64_ref[...], n64_ref[...], cm_ref[...], nm_ref[...]

    def normed(y, seg_ref, p_ref, c0):
        ss = _seg_sumsq(y, seg_ref)
        return y * lax.rsqrt(ss * p_ref[0:1, c0:c0 + MXU_WIDTH] + EPS) * p_ref[1:2, c0:c0 + MXU_WIDTH]

    tiles = lambda c0: [(t, slice(c0 + LANES * t, c0 + LANES * (t + 1))) for t in range(2)]
    latent = {}
    jobs = []

    def epi_cq(y):
        latent["cq"] = normed(y, s256_ref, pin_ref, 0).astype(_MX)
    jobs.append((lambda: _dot(h, win_ref[:, 0:256]), epi_cq))

    def epi_ckv(y):
        z = normed(y, s128_ref, pin_ref, 256)
        latent["ckv"] = z[:, :LANES].astype(_MX)
        latent["k_rope"] = _rope_tile(z[:, LANES:], MLA_ROPE // 2, cm, nm)
    jobs.append((lambda: _dot(h, win_ref[:, 256:512]), epi_ckv))

    for j in range(DIL_COLS // MXU_WIDTH):
        def epi_dil(y, j=j):
            if j % 3 < 2:
                z = normed(y, s64_ref, pin_ref, 512 + MXU_WIDTH * j)
                for t, cols in tiles(MXU_WIDTH * j):
                    dil_ref[:, cols] = _rope_tile(z[:, LANES * t:LANES * (t + 1)], DIL_HD // 2, c64, n64)
            else:
                dil_ref[:, MXU_WIDTH * j:MXU_WIDTH * (j + 1)] = y
        jobs.append((lambda j=j: _dot(h, win_ref[:, 512 + MXU_WIDTH * j:512 + MXU_WIDTH * (j + 1)]), epi_dil))

    for j in range(MLA_HEADS * LANES // MXU_WIDTH):
        c0 = MXU_WIDTH * j
        def epi_q(y, c0=c0):
            z = normed(y, s64_ref, pq_ref, c0)
            for t, cols in tiles(c0):
                qm_ref[:, cols] = _rope_tile(z[:, LANES * t:LANES * (t + 1)], MLA_ROPE // 2, cm, nm).astype(_MX)
        jobs.append((lambda c0=c0: _dot(latent["cq"], wq_ref[:, c0:c0 + MXU_WIDTH]), epi_q))

        def epi_k(y, c0=c0):
            z = normed(y, s64_ref, pk_ref, c0)
            for t, cols in tiles(c0):
                km_ref[:, cols] = (z[:, LANES * t:LANES * (t + 1)] + latent["k_rope"]).astype(_MX)
        jobs.append((lambda c0=c0: _dot(latent["ckv"], wk_ref[:, c0:c0 + MXU_WIDTH]), epi_k))

    for j in range(MLA_HEADS * MLA_V // MXU_WIDTH):
        c0 = MXU_WIDTH * j
        def epi_v(y, c0=c0):
            for t, cols in tiles(c0):
                vt_ref[cols, :] = y[:, LANES * t:LANES * (t + 1)].T.astype(_MX)
        jobs.append((lambda c0=c0: _dot(latent["ckv"], wv_ref[:, c0:c0 + MXU_WIDTH]), epi_v))
    _run_pipelined(jobs)


def _const_spec(shape):
    nd = len(shape)
    return pl.BlockSpec(shape, lambda i: (0,) * nd)


def _even_proj(x, g, win, wq, wk, wv, pin, pq, pk, segs, tabs):
    n = x.shape[0]
    tm = PROJ_TM
    row = lambda w: pl.BlockSpec((tm, w), lambda i: (i, 0))
    s64, s128, s256 = segs
    c64, n64, cm, nm = tabs
    return pl.pallas_call(
        _even_proj_kernel,
        grid=(n // tm,),
        in_specs=[row(D_MODEL), _const_spec(g.shape), _const_spec(win.shape), _const_spec(wq.shape),
                  _const_spec(wk.shape), _const_spec(wv.shape), _const_spec(pin.shape),
                  _const_spec(pq.shape), _const_spec(pk.shape), _const_spec(s64.shape),
                  _const_spec(s128.shape), _const_spec(s256.shape),
                  row(LANES), row(LANES), row(LANES), row(LANES)],
        out_specs=[row(MLA_HEADS * LANES), row(MLA_HEADS * LANES),
                   pl.BlockSpec((None, MLA_HEADS * MLA_V, tm), lambda i: (i, 0, 0)), row(DIL_COLS)],
        out_shape=[jax.ShapeDtypeStruct((n, MLA_HEADS * LANES), _MX),
                   jax.ShapeDtypeStruct((n, MLA_HEADS * LANES), _MX),
                   jax.ShapeDtypeStruct((n // tm, MLA_HEADS * MLA_V, tm), _MX),
                   jax.ShapeDtypeStruct((n, DIL_COLS), _F32)],
        compiler_params=_params(("parallel",)),
        name="even_proj",
    )(x, g, win, wq, wk, wv, pin, pq, pk, s64, s128, s256, c64, n64, cm, nm)


MLA_TQ = 256
MLA_TK = 256


def _mla_attn_kernel(q_ref, k_ref, vt_ref, o_ref, s_scr, m_scr, acc_scr):
    qi = pl.program_id(1)
    scale = (MLA_NOPE + MLA_ROPE) ** -0.5
    m_scr[...] = jnp.full_like(m_scr, _NEG)
    acc_scr[...] = jnp.zeros_like(acc_scr)
    kpos = lax.broadcasted_iota(jnp.int32, (MLA_TK, MLA_TQ), 0)
    qcol = lax.broadcasted_iota(jnp.int32, (MLA_TK, MLA_TQ), 1)

    def score_chunk(c, diagonal):
        k0 = pl.multiple_of(c * MLA_TK, MLA_TK)
        for hd in range(MLA_HEADS):
            cols = slice(LANES * hd, LANES * (hd + 1))
            st = _dot_nt(k_ref[pl.ds(k0, MLA_TK), cols], q_ref[:, cols])
            if diagonal:
                st = jnp.where(kpos <= qcol, st, _NEG)
            s_scr[c, hd] = st
            m_scr[hd] = jnp.maximum(m_scr[hd], jnp.max(st, axis=0, keepdims=True))

    def score_pass(c, carry):
        score_chunk(c, False)
        return carry

    ones = jnp.ones((MLA_V, MLA_TK), _MX)

    def value_pass(c, carry):
        for hd in range(MLA_HEADS):
            p = jnp.exp2((s_scr[c, hd] - m_scr[hd]) * (scale * math.log2(math.e)))
            v1t = jnp.concatenate([vt_ref[c, MLA_V * hd:MLA_V * (hd + 1), :], ones], axis=0)
            acc_scr[hd] += _dot(v1t, p.astype(_MX))
        return carry

    lax.fori_loop(0, qi, score_pass, 0)
    score_chunk(qi, True)
    lax.fori_loop(0, qi + 1, value_pass, 0)
    ot = jnp.concatenate([acc_scr[hd, :MLA_V] / acc_scr[hd, MLA_V:MLA_V + 1] for hd in range(MLA_HEADS)],
                         axis=0)
    o_ref[...] = ot.T.astype(o_ref.dtype)


def _mla_attn(qm, km, vt, batch, seq):
    n = qm.shape[0]
    nq = seq // MLA_TQ
    wv = MLA_HEADS * MLA_V
    return pl.pallas_call(
        _mla_attn_kernel,
        grid=(batch, nq),
        in_specs=[
            pl.BlockSpec((MLA_TQ, MLA_HEADS * LANES), lambda b, i: (b * nq + i, 0)),
            pl.BlockSpec((seq, MLA_HEADS * LANES), lambda b, i: (b, 0)),
            pl.BlockSpec((seq // MLA_TK, wv, MLA_TK), lambda b, i: (b, 0, 0)),
        ],
        out_specs=pl.BlockSpec((MLA_TQ, wv), lambda b, i: (b * nq + i, 0)),
        out_shape=jax.ShapeDtypeStruct((n, wv), _MX),
        scratch_shapes=[pltpu.VMEM((seq // MLA_TK, MLA_HEADS, MLA_TK, MLA_TQ), _F32),
                        pltpu.VMEM((MLA_HEADS, 1, MLA_TQ), _F32),
                        pltpu.VMEM((MLA_HEADS, 2 * MLA_V, MLA_TQ), _F32)],
        compiler_params=_params(("parallel", "arbitrary")),
        name="mla_attn",
    )(qm, km, vt)


DIL_BLK = 128
DIL_UNROLL = 8


def _dil_attn_kernel(*refs, seq):
    n_grp = len(DIL_PATTERNS)
    qkv = [refs[3 * g:3 * g + 3] for g in range(n_grp)]
    out_ref = refs[3 * n_grp]
    o_slabs = refs[3 * n_grp + 1:3 * n_grp + 1 + n_grp]
    l_slabs = refs[3 * n_grp + 1 + n_grp:]
    scale = DIL_HD ** -0.5
    kidx = lax.broadcasted_iota(jnp.int32, (DIL_BLK, DIL_BLK), 0)
    qidx = lax.broadcasted_iota(jnp.int32, (DIL_BLK, DIL_BLK), 1)
    lane = lax.broadcasted_iota(jnp.int32, (DIL_BLK, LANES), 1)
    dim = lax.broadcasted_iota(jnp.int32, (LANES, DIL_BLK), 0)
    cur_ok = kidx <= qidx

    for g, (window, d) in enumerate(DIL_PATTERNS):
        q_ref, k_ref, v_ref = qkv[g]
        nb = seq // d // DIL_BLK

        def units(it, carry, d=d, nb=nb, q_ref=q_ref, k_ref=k_ref, v_ref=v_ref, g=g):
            staged = []
            for u in range(DIL_UNROLL):
                idx = it * DIL_UNROLL + u
                r = idx // nb
                n = idx % nb
                rows = pl.ds(r + d * DIL_BLK * n, DIL_BLK, stride=d)
                prev = pl.ds(r + d * DIL_BLK * jnp.maximum(n - 1, 0), DIL_BLK, stride=d)
                prev_ok = kidx >= qidx + jnp.where(n > 0, 0, DIL_BLK)
                q2 = q_ref[rows, :] * scale
                kc = k_ref[rows, :].astype(_MX)
                kp = k_ref[prev, :].astype(_MX)
                halves = []
                for half in range(2):
                    mine = (lane < DIL_HD) if half == 0 else (lane >= DIL_HD)
                    qh = jnp.where(mine, q2, 0.0).astype(_MX)
                    halves.append((_dot_nt(kc, qh), _dot_nt(kp, qh)))
                staged.append((rows, prev, prev_ok, halves))
            for rows, prev, prev_ok, halves in staged:
                vct = v_ref[rows, :].T.astype(_MX)
                vpt = v_ref[prev, :].T.astype(_MX)
                ots, lses = [], []
                for sc, sp in halves:
                    sc = jnp.where(cur_ok, sc, _NEG)
                    sp = jnp.where(prev_ok, sp, _NEG)
                    m = jnp.maximum(jnp.max(sc, axis=0, keepdims=True), jnp.max(sp, axis=0, keepdims=True))
                    pc = jnp.exp(sc - m)
                    pp = jnp.exp(sp - m)
                    l = jnp.sum(pc, axis=0, keepdims=True) + jnp.sum(pp, axis=0, keepdims=True)
                    ots.append((_dot(vct, pc.astype(_MX)) + _dot(vpt, pp.astype(_MX))) / l)
                    lses.append(m + jnp.log(l))
                ot = jnp.where(dim < DIL_HD, ots[0], ots[1])
                lt = jnp.where(dim < DIL_HD, lses[0], lses[1])
                o_slabs[g][rows, :] = ot.T
                l_slabs[g][rows, :] = lt.T
            return carry

        lax.fori_loop(0, d * nb // DIL_UNROLL, units, 0)

    ls = [l_slabs[g][...] for g in range(n_grp)]
    m = functools.reduce(jnp.maximum, ls)
    es = [jnp.exp(l - m) for l in ls]
    num = sum(e * o_slabs[g][...] for g, e in enumerate(es))
    out_ref[...] = (num / sum(es)).astype(out_ref.dtype)


def _dil_attn(dil, batch, seq):
    n = dil.shape[0]
    pairs = DIL_HEADS // 2
    spec = lambda g, part: pl.BlockSpec((seq, LANES), lambda b, p: (b, (3 * g + part) * pairs + p))
    in_specs = [spec(g, part) for g in range(len(DIL_PATTERNS)) for part in range(3)]
    return pl.pallas_call(
        functools.partial(_dil_attn_kernel, seq=seq),
        grid=(batch, pairs),
        in_specs=in_specs,
        out_specs=pl.BlockSpec((seq, LANES), lambda b, p: (b, p)),
        out_shape=jax.ShapeDtypeStruct((n, DIL_HEADS * DIL_HD), _MX),
        scratch_shapes=[pltpu.VMEM((seq, LANES), _F32)] * (2 * len(DIL_PATTERNS)),
        compiler_params=_params(("parallel", "parallel")),
        name="dil_attn",
    )(*([dil] * len(in_specs)))


ODD_IN_PAD = 2048


def _odd_proj_kernel(x_ref, g_ref, win_ref, pin_ref, s64_ref, c64_ref, n64_ref,
                     q_ref, kk_ref, vt_ref, qi_ref, ki_ref, wit_ref):
    h = _row_rmsnorm(x_ref[...], g_ref[...]).astype(_MX)
    c64, n64 = c64_ref[...], n64_ref[...]

    def normed(y, c0):
        ss = _seg_sumsq(y, s64_ref)
        return y * lax.rsqrt(ss * pin_ref[0:1, c0:c0 + MXU_WIDTH] + EPS) * pin_ref[1:2, c0:c0 + MXU_WIDTH]

    def rope64(t):
        return _rope_tile(t, SPA_HD // 2, c64, n64)

    nq = SPA_HEADS * SPA_HD
    c_qi = nq + MXU_WIDTH
    c_ki = c_qi + IDX_HEADS * IDX_HD
    product = lambda c0: (lambda: _dot(h, win_ref[:, c0:c0 + MXU_WIDTH]))
    jobs = []
    for j in range(nq // MXU_WIDTH):
        def epi_q(y, c0=MXU_WIDTH * j):
            z = normed(y, c0)
            for t in range(2):
                q_ref[:, c0 + LANES * t:c0 + LANES * (t + 1)] = rope64(z[:, LANES * t:LANES * (t + 1)]).astype(_MX)
        jobs.append((product(MXU_WIDTH * j), epi_q))

    def epi_kv(y):
        z = normed(y, nq)
        kk_ref[...] = rope64(z[:, :LANES]).astype(_MX)
        vt_ref[...] = y[:, LANES:].T.astype(_MX)
    jobs.append((product(nq), epi_kv))

    for j in range(IDX_HEADS * IDX_HD // MXU_WIDTH):
        def epi_qi(y, o0=MXU_WIDTH * j):
            for t in range(2):
                qi_ref[:, o0 + LANES * t:o0 + LANES * (t + 1)] = rope64(y[:, LANES * t:LANES * (t + 1)]).astype(_MX)
        jobs.append((product(c_qi + MXU_WIDTH * j), epi_qi))

    def epi_ki(y):
        ki_ref[...] = rope64(y[:, :LANES]).astype(_MX)
        wi = (y[:, LANES:] * (IDX_HEADS ** -0.5)) * (IDX_HD ** -0.5)
        wit_ref[...] = wi.T[0:IDX_HEADS, :]
    jobs.append((product(c_ki), epi_ki))
    _run_pipelined(jobs)


def _odd_proj(x, g, win, pin, s64, tabs):
    n = x.shape[0]
    tm = PROJ_TM
    row = lambda w: pl.BlockSpec((tm, w), lambda i: (i, 0))
    c64, n64 = tabs
    return pl.pallas_call(
        _odd_proj_kernel,
        grid=(n // tm,),
        in_specs=[row(D_MODEL), _const_spec(g.shape), _const_spec(win.shape), _const_spec(pin.shape),
                  _const_spec(s64.shape), row(LANES), row(LANES)],
        out_specs=[row(SPA_HEADS * SPA_HD), row(LANES),
                   pl.BlockSpec((None, LANES, tm), lambda i: (i, 0, 0)),
                   row(IDX_HEADS * IDX_HD), row(LANES),
                   pl.BlockSpec((IDX_HEADS, tm), lambda i: (0, i))],
        out_shape=[jax.ShapeDtypeStruct((n, SPA_HEADS * SPA_HD), _MX),
                   jax.ShapeDtypeStruct((n, LANES), _MX),
                   jax.ShapeDtypeStruct((n // tm, LANES, tm), _MX),
                   jax.ShapeDtypeStruct((n, IDX_HEADS * IDX_HD), _MX),
                   jax.ShapeDtypeStruct((n, LANES), _MX),
                   jax.ShapeDtypeStruct((IDX_HEADS, n), _F32)],
        compiler_params=_params(("parallel",)),
        name="odd_proj",
    )(x, g, win, pin, s64, c64, n64)


SPA_TQ = 128
SPA_TK = 256
_INT_MIN = -2 ** 31
assert SPA_TK == PROJ_TM and MLA_TK == PROJ_TM and MLA_TQ == MLA_TK


def _sparse_attn_kernel(q_ref, kk_ref, vt_ref, qi_ref, ki_ref, wit_ref, o_ref,
                        key_scr, qs_scr, s_scr, m_scr, acc_scr, *, n_keep):
    qb = pl.program_id(1)
    n_chunks = qb // (SPA_TK // SPA_TQ) + 1
    lane = lax.broadcasted_iota(jnp.int32, (SPA_TQ, LANES), 1)
    left = lane < SPA_HD

    def stack_heads(ref, n_pairs):
        parts = []
        for pr in range(n_pairs):
            t = ref[:, LANES * pr:LANES * (pr + 1)]
            parts.append(jnp.where(left, t, jnp.zeros_like(t)))
            parts.append(jnp.where(left, jnp.zeros_like(t), t))
        return jnp.concatenate(parts, axis=0)

    qi_stack = stack_heads(qi_ref, IDX_HEADS // 2)
    wit = wit_ref[...]
    krow = lax.broadcasted_iota(jnp.int32, (SPA_TK, SPA_TQ), 0)
    qpos = qb * SPA_TQ + lax.broadcasted_iota(jnp.int32, (SPA_TK, SPA_TQ), 1)

    def score_chunk(c, carry):
        k0 = pl.multiple_of(c * SPA_TK, SPA_TK)
        logits = _dot_nt(ki_ref[pl.ds(k0, SPA_TK), :], qi_stack)
        sc = jnp.zeros((SPA_TK, SPA_TQ), _F32)
        for hd in range(IDX_HEADS):
            sc = sc + wit[hd:hd + 1, :] * jnp.maximum(logits[:, SPA_TQ * hd:SPA_TQ * (hd + 1)], 0.0)
        sc = jnp.where((k0 + krow) <= qpos, sc, -jnp.inf)
        bits = pltpu.bitcast(sc, jnp.int32)
        key_scr[c] = bits ^ ((bits >> 31) & jnp.int32(0x7FFFFFFF))
        return carry

    lax.fori_loop(0, n_chunks, score_chunk, 0)

    def count(pred):
        def body(c, acc):
            hit = pred(key_scr[c], c * SPA_TK)
            return acc + jnp.sum(hit.astype(jnp.int32).reshape(SPA_TK // 8, 8, SPA_TQ), axis=0)
        acc = lax.fori_loop(0, n_chunks, body, jnp.zeros((8, SPA_TQ), jnp.int32))
        return jnp.sum(acc, axis=0, keepdims=True)

    def refine(t, cand):
        return jnp.where(count(lambda k, k0: k >= cand) >= n_keep, cand, t)

    t = jnp.full((1, SPA_TQ), _INT_MIN, jnp.int32)
    t = refine(t, jnp.zeros((1, SPA_TQ), jnp.int32))
    t = lax.fori_loop(0, 31, lambda i, t: refine(t, t | (jnp.int32(1) << (30 - i))), t)
    n_gt = count(lambda k, k0: k > t)
    n_ge = count(lambda k, k0: k >= t)
    need = n_keep - n_gt
    n_bits = (key_scr.shape[0] * SPA_TK).bit_length()

    def search_last():
        def refine_pos(j, cand):
            below = count(lambda k, k0: (k == t) & ((k0 + krow) < cand))
            return jnp.where(below < need, cand, j)
        return lax.fori_loop(0, n_bits, lambda i, j: refine_pos(j, j | (jnp.int32(1) << (n_bits - 1 - i))),
                             jnp.zeros((1, SPA_TQ), jnp.int32))

    keep_all_ties = lambda: jnp.full((1, SPA_TQ), 2 ** n_bits, jnp.int32)
    last = lax.cond(jnp.max(n_ge) > n_keep, search_last, keep_all_ties)

    qs_scr[...] = stack_heads(q_ref, SPA_HEADS // 2) * (SPA_HD ** -0.5)
    m_scr[...] = jnp.full_like(m_scr, _NEG)
    acc_scr[...] = jnp.zeros_like(acc_scr)
    vrow = lax.broadcasted_iota(jnp.int32, (LANES, SPA_TK), 0)
    pair_w = 2 * SPA_TQ
    pairs = [slice(pair_w * pr, pair_w * (pr + 1)) for pr in range(SPA_HEADS // 2)]

    def score_pass(c, carry):
        k0 = pl.multiple_of(c * SPA_TK, SPA_TK)
        keys = key_scr[c]
        kpos = k0 + krow
        chosen = ((keys > t) | ((keys == t) & (kpos <= last))) & (kpos <= qpos)
        bias = jnp.where(chosen, 0.0, _NEG)
        bias2 = jnp.concatenate([bias, bias], axis=1)
        kk = kk_ref[pl.ds(k0, SPA_TK), :]
        for cols in pairs:
            st = _dot_nt(kk, qs_scr[cols, :]) + bias2
            s_scr[c, :, cols] = st
            m_scr[:, cols] = jnp.maximum(m_scr[:, cols], jnp.max(st, axis=0, keepdims=True))
        return carry

    def value_pass(c, carry):
        vt = vt_ref[c]
        v1t = jnp.where(vrow < SPA_HD, vt, jnp.ones_like(vt))
        for cols in pairs:
            p = jnp.exp(s_scr[c, :, cols] - m_scr[:, cols])
            acc_scr[:, cols] += _dot(v1t, p.astype(_MX))
        return carry

    lax.fori_loop(0, n_chunks, score_pass, 0)
    lax.fori_loop(0, n_chunks, value_pass, 0)
    for pr in range(SPA_HEADS // 2):
        a = acc_scr[:, pair_w * pr:pair_w * pr + SPA_TQ]
        b = acc_scr[:, pair_w * pr + SPA_TQ:pair_w * (pr + 1)]
        tile_t = jnp.concatenate([a[:SPA_HD] / a[SPA_HD:SPA_HD + 1], b[:SPA_HD] / b[SPA_HD:SPA_HD + 1]], axis=0)
        o_ref[:, LANES * pr:LANES * (pr + 1)] = tile_t.T.astype(o_ref.dtype)


def _sparse_attn(q, kk, vt, qi, ki, wit, batch, seq):
    n = q.shape[0]
    nq = seq // SPA_TQ
    n_keep = min(TOPK_MAX, seq // 4)
    return pl.pallas_call(
        functools.partial(_sparse_attn_kernel, n_keep=n_keep),
        grid=(batch, nq),
        in_specs=[
            pl.BlockSpec((SPA_TQ, SPA_HEADS * SPA_HD), lambda b, i: (b * nq + i, 0)),
            pl.BlockSpec((seq, LANES), lambda b, i: (b, 0)),
            pl.BlockSpec((seq // SPA_TK, LANES, SPA_TK), lambda b, i: (b, 0, 0)),
            pl.BlockSpec((SPA_TQ, IDX_HEADS * IDX_HD), lambda b, i: (b * nq + i, 0)),
            pl.BlockSpec((seq, LANES), lambda b, i: (b, 0)),
            pl.BlockSpec((IDX_HEADS, SPA_TQ), lambda b, i: (0, b * nq + i)),
        ],
        out_specs=pl.BlockSpec((SPA_TQ, SPA_HEADS * SPA_HD), lambda b, i: (b * nq + i, 0)),
        out_shape=jax.ShapeDtypeStruct((n, SPA_HEADS * SPA_HD), _MX),
        scratch_shapes=[pltpu.VMEM((seq // SPA_TK, SPA_TK, SPA_TQ), jnp.int32),
                        pltpu.VMEM((SPA_HEADS * SPA_TQ, LANES), _MX),
                        pltpu.VMEM((seq // SPA_TK, SPA_TK, SPA_HEADS * SPA_TQ), _F32),
                        pltpu.VMEM((1, SPA_HEADS * SPA_TQ), _F32),
                        pltpu.VMEM((LANES, SPA_HEADS * SPA_TQ), _F32)],
        compiler_params=_params(("parallel", "arbitrary")),
        name="sparse_attn",
    )(q, kk, vt, qi, ki, wit)


def _mem_kv_kernel(mem_ref, g_ref, w_ref, kg_ref, k_ref, v_ref):
    h = _row_rmsnorm(mem_ref[...], g_ref[...]).astype(_MX)
    kg = kg_ref[...]
    for hd in range(X_HEADS):
        cols = slice(X_HD * hd, X_HD * (hd + 1))
        y = _dot(h, w_ref[:, cols])
        k_ref[:, cols] = _row_rmsnorm(y, kg).astype(_MX)
    for hd in range(X_HEADS):
        cols = slice(X_HD * hd, X_HD * (hd + 1))
        v_ref[:, cols] = _dot(h, w_ref[:, D_MODEL + X_HD * hd:D_MODEL + X_HD * (hd + 1)]).astype(_MX)


def _mem_kv(mem, g, wkv, kg):
    n, m = mem.shape[0], 256
    row = pl.BlockSpec((m, D_MODEL), lambda i: (i, 0))
    return pl.pallas_call(
        _mem_kv_kernel,
        grid=(n // m,),
        in_specs=[row, _const_spec(g.shape), _const_spec(wkv.shape), _const_spec(kg.shape)],
        out_specs=[row, row],
        out_shape=[jax.ShapeDtypeStruct((n, D_MODEL), _MX)] * 2,
        compiler_params=_params(("parallel",)),
        name="mem_kv",
    )(mem, g, wkv, kg)


XATT_TM = 512


def _xattn_kernel(*refs, n_mix):
    x_ref = refs[0]
    mix_refs = refs[1:1 + n_mix]
    mixw_refs = refs[1 + n_mix:1 + 2 * n_mix]
    g_ref, wq_ref, qg_ref, k_ref, v_ref, wo_ref, out_ref, o_scr = refs[1 + 2 * n_mix:]
    x = x_ref[...]
    for a_ref, w_ref in zip(mix_refs, mixw_refs):
        x = x + _dot(a_ref[...], w_ref[...])
    h = _row_rmsnorm(x, g_ref[...]).astype(_MX)
    qg = qg_ref[...]
    scale = X_HD ** -0.5
    heads = [slice(X_HD * hd, X_HD * (hd + 1)) for hd in range(X_HEADS)]
    qs = [_dot(h, wq_ref[:, cols]) for cols in heads]
    qs = [_row_rmsnorm(q, qg).astype(_MX) for q in qs]
    ss = [_dot_nt(q, k_ref[:, cols]) * scale for q, cols in zip(qs, heads)]
    for s, cols in zip(ss, heads):
        p = jnp.exp(s - jnp.max(s, axis=-1, keepdims=True))
        p = p / jnp.sum(p, axis=-1, keepdims=True)
        o_scr[:, cols] = _dot(p.astype(_MX), v_ref[:, cols]).astype(_MX)
    out_ref[...] = x + _dot(o_scr[...], wo_ref[...])


def _xattn(x, mix, mix_w, g, wq, qg, k, v, wo, batch, seq, mem_len):
    n = x.shape[0]
    tm = XATT_TM
    nt = seq // tm
    row = lambda w: pl.BlockSpec((tm, w), lambda b, i: (b * nt + i, 0))
    const = lambda a: pl.BlockSpec(a.shape, lambda b, i: (0,) * a.ndim)
    memspec = pl.BlockSpec((mem_len, D_MODEL), lambda b, i: (b, 0))
    return pl.pallas_call(
        functools.partial(_xattn_kernel, n_mix=len(mix)),
        grid=(batch, nt),
        in_specs=([row(D_MODEL)] + [row(a.shape[1]) for a in mix] + [const(w) for w in mix_w]
                  + [const(g), const(wq), const(qg), memspec, memspec, const(wo)]),
        out_specs=row(D_MODEL),
        out_shape=jax.ShapeDtypeStruct(x.shape, _F32),
        scratch_shapes=[pltpu.VMEM((tm, D_MODEL), _MX)],
        compiler_params=_params(("parallel", "parallel")),
        name="xattn",
    )(x, *mix, *mix_w, g, wq, qg, k, v, wo)


def _seg_matrix(seg):
    idx = np.arange(MXU_WIDTH) // seg
    return jnp.asarray(idx[:, None] == idx[None, :], _MX)


def _rope_tables(positions, dim):
    inv = jnp.exp(-math.log(ROPE_THETA) * jnp.arange(0, dim, 2, dtype=_F32) / dim)
    ang = positions.astype(_F32).reshape(-1)[:, None] * inv
    return jnp.cos(ang), jnp.sin(ang)


def _tables(positions):
    cos64, sin64 = _rope_tables(positions, DIL_HD)
    cos32, sin32 = _rope_tables(positions, MLA_ROPE)
    n = cos64.shape[0]
    c64 = jnp.tile(cos64, (1, 4))
    n64 = jnp.tile(jnp.concatenate([-sin64, sin64], axis=1), (1, 2))
    ones, zeros = jnp.ones((n, MLA_NOPE), _F32), jnp.zeros((n, MLA_NOPE), _F32)
    pad1, pad0 = jnp.ones((n, 32), _F32), jnp.zeros((n, 32), _F32)
    cm = jnp.concatenate([ones, cos32, cos32, pad1], axis=1)
    nm = jnp.concatenate([zeros, -sin32, sin32, pad0], axis=1)
    return c64, n64, cm, nm


def _even_weights(w_in, q_lora_g, kv_lora_g, w_uq, w_ukv, q_g, k_g, dq_g, dk_g, w_out):
    o1, o2, o3 = MLA_Q_LORA, MLA_Q_LORA + MLA_KV_LORA, MLA_Q_LORA + MLA_KV_LORA + MLA_ROPE
    d = w_in.shape[0]
    z = lambda w: jnp.zeros((d, w), _F32)
    win = jnp.concatenate([w_in[:, :o2], z(64), w_in[:, o2:o3], z(32), w_in[:, o3:]], axis=1).astype(_MX)
    ones = lambda w: jnp.ones((w,), _F32)
    zeros = lambda w: jnp.zeros((w,), _F32)
    grp_gain = jnp.concatenate([jnp.tile(dq_g, DIL_HEADS), jnp.tile(dk_g, DIL_HEADS), ones(256)])
    pin = jnp.stack([
        jnp.concatenate([ones(256) / 256, ones(128) / 128, ones(128) / MLA_ROPE, ones(DIL_COLS) / DIL_HD]),
        jnp.concatenate([q_lora_g, kv_lora_g, zeros(64), k_g[MLA_NOPE:], zeros(32), jnp.tile(grp_gain, 3)]),
    ])
    wq = jnp.pad(w_uq.reshape(MLA_Q_LORA, MLA_HEADS, MLA_NOPE + MLA_ROPE), ((0, 0), (0, 0), (0, 32)))
    wq = wq.reshape(MLA_Q_LORA, MLA_HEADS * LANES).astype(_MX)
    pq = jnp.stack([
        jnp.tile(jnp.concatenate([ones(64) / MLA_NOPE, ones(64) / MLA_ROPE]), MLA_HEADS),
        jnp.tile(jnp.concatenate([q_g, zeros(32)]), MLA_HEADS),
    ])
    ukv = w_ukv.reshape(MLA_KV_LORA, MLA_HEADS, MLA_NOPE + MLA_V)
    wk = jnp.pad(ukv[:, :, :MLA_NOPE], ((0, 0), (0, 0), (0, 64))).reshape(MLA_KV_LORA, MLA_HEADS * LANES).astype(_MX)
    wv = ukv[:, :, MLA_NOPE:].reshape(MLA_KV_LORA, MLA_HEADS * MLA_V).astype(_MX)
    pk = jnp.stack([
        jnp.tile(ones(LANES) / MLA_NOPE, MLA_HEADS),
        jnp.tile(jnp.concatenate([k_g[:MLA_NOPE], zeros(64)]), MLA_HEADS),
    ])
    return win, wq, wk, wv, pin, pq, pk, w_out.astype(_MX)


def _odd_weights(w_in, q_g, k_g, w_out):
    o1 = SPA_HEADS * SPA_HD
    o2, o3 = o1 + SPA_HD, o1 + 2 * SPA_HD
    o4 = o3 + IDX_HEADS * IDX_HD
    o5 = o4 + IDX_HD
    d = w_in.shape[0]
    k, v, ki = w_in[:, o1:o2], w_in[:, o2:o3], w_in[:, o4:o5]
    win = jnp.concatenate([w_in[:, :o1], k, k, v, v, w_in[:, o3:o4], ki, ki, w_in[:, o5:],
                           jnp.zeros((d, LANES - IDX_HEADS), _F32)], axis=1).astype(_MX)
    ones = jnp.ones((ODD_IN_PAD,), _F32)
    gain = jnp.concatenate([jnp.tile(q_g, SPA_HEADS), k_g, k_g, jnp.ones((ODD_IN_PAD - o1 - 2 * SPA_HD,), _F32)])
    pin = jnp.stack([ones / SPA_HD, gain])
    return win, pin, w_out.astype(_MX)


def kernel(x, mem, positions, ffn1_norm, ffn1_w13, ffn1_w2, mix_norm, xattn_norm, mem_norm,
           xattn_wq, xattn_wkv, xattn_q_gain, xattn_k_gain, xattn_wo, ffn2_norm, ffn2_w13, ffn2_w2,
           even_w_in, mla_q_lora_norm, mla_kv_lora_norm, mla_w_uq, mla_w_ukv, mla_q_gain, mla_k_gain,
           dil_q_gain, dil_k_gain, even_w_out, odd_w_in, sparse_q_gain, sparse_k_gain, odd_w_out):
    batch, seq, d = x.shape
    mem_len = mem.shape[1]
    n = batch * seq
    c64, n64, cm, nm = _tables(positions)
    s64, s128, s256 = _seg_matrix(64), _seg_matrix(128), _seg_matrix(256)
    bf = lambda a: a.astype(_MX)
    row3 = lambda a: a.reshape(a.shape[0], 1, a.shape[1])
    f1w13, f1w2, f2w13, f2w2 = bf(ffn1_w13), bf(ffn1_w2), bf(ffn2_w13), bf(ffn2_w2)
    f1g, f2g = row3(ffn1_norm), row3(ffn2_norm)
    xwq, xwkv, xwo = bf(xattn_wq), bf(xattn_wkv), bf(xattn_wo)

    xs = x.reshape(n, d)
    mems = mem.reshape(batch * mem_len, d)
    for i in range(DEPTH):
        j = i // 2
        xs = _ffn(xs, f1g, f1w13, f1w2, i)
        g_mix = mix_norm[i][None, :]
        if i % 2 == 0:
            win, wq, wk, wv, pin, pq, pk, wout = _even_weights(
                even_w_in[j], mla_q_lora_norm[j], mla_kv_lora_norm[j], mla_w_uq[j], mla_w_ukv[j],
                mla_q_gain[j], mla_k_gain[j], dil_q_gain[j], dil_k_gain[j], even_w_out[j])
            qm, km, vt, dil = _even_proj(xs, g_mix, win, wq, wk, wv, pin, pq, pk,
                                         (s64, s128, s256), (c64, n64, cm, nm))
            n_mla = MLA_HEADS * MLA_V
            mix = [_mla_attn(qm, km, vt, batch, seq), _dil_attn(dil, batch, seq)]
            mix_w = [wout[:n_mla], wout[n_mla:]]
        else:
            win, pin, wout = _odd_weights(odd_w_in[j], sparse_q_gain[j], sparse_k_gain[j], odd_w_out[j])
            q, kk, vt, qi, ki, wit = _odd_proj(xs, g_mix, win, pin, s64, (c64, n64))
            mix = [_sparse_attn(q, kk, vt, qi, ki, wit, batch, seq)]
            mix_w = [wout]
        mk, mv = _mem_kv(mems, mem_norm[i][None, :], xwkv[i], xattn_k_gain[i][None, :])
        xs = _xattn(xs, mix, mix_w, xattn_norm[i][None, :], xwq[i], xattn_q_gain[i][None, :], mk, mv, xwo[i],
                    batch, seq, mem_len)
        xs = _ffn(xs, f2g, f2w13, f2w2, i)
    return xs.reshape(batch, seq, d)
```

```python
import functools
import math

import jax
import jax.numpy as jnp
import numpy as np
from jax import lax
from jax.experimental import pallas as pl
from jax.experimental.pallas import tpu as pltpu

D_MODEL = 1024
DEPTH = 4
EPS = 1e-6
D_FF = 2816
ROPE_THETA = 10000.0
MLA_HEADS, MLA_NOPE, MLA_ROPE, MLA_V = 8, 64, 32, 64
MLA_Q_LORA, MLA_KV_LORA = 256, 128
DIL_PATTERNS = ((128, 1), (512, 4), (2048, 16))
DIL_HEADS, DIL_HD = 4, 64
SPA_HEADS, SPA_HD = 16, 64
IDX_HEADS, IDX_HD = 8, 64
TOPK_MAX = 256
X_HEADS = 4
X_HD = D_MODEL // X_HEADS

LANES = 128
MXU_WIDTH = 256
VMEM_LIMIT_BYTES = 56 * 1024 * 1024

_MX = jnp.bfloat16
_F32 = jnp.float32
_NEG = -1e30


def _dot(a, b):
    return jnp.dot(a, b, preferred_element_type=_F32)


def _dot_nt(a, b):
    return lax.dot_general(a, b, (((1,), (1,)), ((), ())), preferred_element_type=_F32)


def _params(sem, flags=None):
    return pltpu.CompilerParams(dimension_semantics=sem, vmem_limit_bytes=VMEM_LIMIT_BYTES, flags=flags)


def _row_rmsnorm(x, g):
    return x * lax.rsqrt(jnp.mean(x * x, axis=-1, keepdims=True) + EPS) * g


def _seg_sumsq(y, seg_ref):
    y2 = y * y
    hi = y2.astype(_MX)
    lo = (y2 - hi.astype(_F32)).astype(_MX)
    s = seg_ref[...]
    return _dot(hi, s) + _dot(lo, s)


def _swap_lanes(z, d):
    lane = lax.broadcasted_iota(jnp.int32, z.shape, 1)
    fwd = pltpu.roll(z, LANES - d, 1)
    bwd = pltpu.roll(z, d, 1)
    return jnp.where((lane & d) == 0, fwd, bwd)


def _rope_tile(z, d, cos, sin_signed):
    return z * cos + _swap_lanes(z, d) * sin_signed


def _run_pipelined(jobs):
    y_next = jobs[0][0]()
    for i, (_, epilogue) in enumerate(jobs):
        y = y_next
        if i + 1 < len(jobs):
            y_next = jobs[i + 1][0]()
        epilogue(y)


FFN_TM = 512
FFN_CHUNKS = (768, 768, 768, 512)
assert sum(FFN_CHUNKS) == D_FF and all(c % MXU_WIDTH == 0 for c in FFN_CHUNKS)


def _ffn_kernel(x_ref, g_ref, w13_ref, w2_ref, o_ref, act_scr):
    x = x_ref[...]
    h = _row_rmsnorm(x, g_ref[...]).astype(_MX)
    c0 = 0
    for width in FFN_CHUNKS:
        gate = _dot(h, w13_ref[:, c0:c0 + width])
        up = _dot(h, w13_ref[:, D_FF + c0:D_FF + c0 + width])
        act_scr[:, c0:c0 + width] = (jax.nn.silu(gate) * up).astype(_MX)
        c0 += width
    o_ref[...] = x + 0.5 * _dot(act_scr[...], w2_ref[...])


def _ffn(x, g, w13, w2, layer):
    n = x.shape[0]
    resident = pl.Buffered(1)
    return pl.pallas_call(
        _ffn_kernel,
        grid=(n // FFN_TM,),
        in_specs=[
            pl.BlockSpec((FFN_TM, D_MODEL), lambda i: (i, 0)),
            pl.BlockSpec((None, 1, D_MODEL), lambda i: (layer, 0, 0)),
            pl.BlockSpec((None, D_MODEL, 2 * D_FF), lambda i: (layer, 0, 0), pipeline_mode=resident),
            pl.BlockSpec((None, D_FF, D_MODEL), lambda i: (layer, 0, 0), pipeline_mode=resident),
        ],
        out_specs=pl.BlockSpec((FFN_TM, D_MODEL), lambda i: (i, 0)),
        out_shape=jax.ShapeDtypeStruct(x.shape, _F32),
        scratch_shapes=[pltpu.VMEM((FFN_TM, D_FF), _MX)],
        compiler_params=_params(("parallel",)),
        name="ffn",
    )(x, g, w13, w2)


PROJ_TM = 256
EVEN_IN_PAD = 2816
DIL_COLS = 2304


def _even_proj_kernel(x_ref, g_ref, win_ref, wq_ref, wk_ref, wv_ref, pin_ref, pq_ref, pk_ref,
                      s64_ref, s128_ref, s256_ref, c64_ref, n64_ref, cm_ref, nm_ref,
                      qm_ref, km_ref, vt_ref, dil_ref):
    h = _row_rmsnorm(x_ref[...], g_ref[...]).astype(_MX)
    c64, n64, cm, nm = c64_ref[...], n64_ref[...], cm_ref[...], nm_ref[...]

    def normed(y, seg_ref, p_ref, c0):
        ss = _seg_sumsq(y, seg_ref)
        return y * lax.rsqrt(ss * p_ref[0:1, c0:c0 + MXU_WIDTH] + EPS) * p_ref[1:2, c0:c0 + MXU_WIDTH]

    tiles = lambda c0: [(t, slice(c0 + LANES * t, c0 + LANES * (t + 1))) for t in range(2)]
    latent = {}
    jobs = []

    def epi_cq(y):
        latent["cq"] = normed(y, s256_ref, pin_ref, 0).astype(_MX)
    jobs.append((lambda: _dot(h, win_ref[:, 0:256]), epi_cq))

    def epi_ckv(y):
        z = normed(y, s128_ref, pin_ref, 256)
        latent["ckv"] = z[:, :LANES].astype(_MX)
        latent["k_rope"] = _rope_tile(z[:, LANES:], MLA_ROPE // 2, cm, nm)
    jobs.append((lambda: _dot(h, win_ref[:, 256:512]), epi_ckv))

    for j in range(DIL_COLS // MXU_WIDTH):
        def epi_dil(y, j=j):
            if j % 3 < 2:
                z = normed(y, s64_ref, pin_ref, 512 + MXU_WIDTH * j)
                for t, cols in tiles(MXU_WIDTH * j):
                    dil_ref[:, cols] = _rope_tile(z[:, LANES * t:LANES * (t + 1)], DIL_HD // 2, c64, n64)
            else:
                dil_ref[:, MXU_WIDTH * j:MXU_WIDTH * (j + 1)] = y
        jobs.append((lambda j=j: _dot(h, win_ref[:, 512 + MXU_WIDTH * j:512 + MXU_WIDTH * (j + 1)]), epi_dil))

    for j in range(MLA_HEADS * LANES // MXU_WIDTH):
        c0 = MXU_WIDTH * j
        def epi_q(y, c0=c0):
            z = normed(y, s64_ref, pq_ref, c0)
            for t, cols in tiles(c0):
                qm_ref[:, cols] = _rope_tile(z[:, LANES * t:LANES * (t + 1)], MLA_ROPE // 2, cm, nm).astype(_MX)
        jobs.append((lambda c0=c0: _dot(latent["cq"], wq_ref[:, c0:c0 + MXU_WIDTH]), epi_q))

        def epi_k(y, c0=c0):
            z = normed(y, s64_ref, pk_ref, c0)
            for t, cols in tiles(c0):
                km_ref[:, cols] = (z[:, LANES * t:LANES * (t + 1)] + latent["k_rope"]).astype(_MX)
        jobs.append((lambda c0=c0: _dot(latent["ckv"], wk_ref[:, c0:c0 + MXU_WIDTH]), epi_k))

    for j in range(MLA_HEADS * MLA_V // MXU_WIDTH):
        c0 = MXU_WIDTH * j
        def epi_v(y, c0=c0):
            for t, cols in tiles(c0):
                vt_ref[cols, :] = y[:, LANES * t:LANES * (t + 1)].T.astype(_MX)
        jobs.append((lambda c0=c0: _dot(latent["ckv"], wv_ref[:, c0:c0 + MXU_WIDTH]), epi_v))
    _run_pipelined(jobs)


def _const_spec(shape):
    nd = len(shape)
    return pl.BlockSpec(shape, lambda i: (0,) * nd)


def _even_proj(x, g, win, wq, wk, wv, pin, pq, pk, segs, tabs):
    n = x.shape[0]
    tm = PROJ_TM
    row = lambda w: pl.BlockSpec((tm, w), lambda i: (i, 0))
    s64, s128, s256 = segs
    c64, n64, cm, nm = tabs
    return pl.pallas_call(
        _even_proj_kernel,
        grid=(n // tm,),
        in_specs=[row(D_MODEL), _const_spec(g.shape), _const_spec(win.shape), _const_spec(wq.shape),
                  _const_spec(wk.shape), _const_spec(wv.shape), _const_spec(pin.shape),
                  _const_spec(pq.shape), _const_spec(pk.shape), _const_spec(s64.shape),
                  _const_spec(s128.shape), _const_spec(s256.shape),
                  row(LANES), row(LANES), row(LANES), row(LANES)],
        out_specs=[row(MLA_HEADS * LANES), row(MLA_HEADS * LANES),
                   pl.BlockSpec((None, MLA_HEADS * MLA_V, tm), lambda i: (i, 0, 0)), row(DIL_COLS)],
        out_shape=[jax.ShapeDtypeStruct((n, MLA_HEADS * LANES), _MX),
                   jax.ShapeDtypeStruct((n, MLA_HEADS * LANES), _MX),
                   jax.ShapeDtypeStruct((n // tm, MLA_HEADS * MLA_V, tm), _MX),
                   jax.ShapeDtypeStruct((n, DIL_COLS), _F32)],
        compiler_params=_params(("parallel",)),
        name="even_proj",
    )(x, g, win, wq, wk, wv, pin, pq, pk, s64, s128, s256, c64, n64, cm, nm)


MLA_TQ = 256
MLA_TK = 256


def _mla_attn_kernel(q_ref, k_ref, vt_ref, o_ref, s_scr, m_scr, acc_scr):
    qi = pl.program_id(1)
    scale = (MLA_NOPE + MLA_ROPE) ** -0.5
    m_scr[...] = jnp.full_like(m_scr, _NEG)
    acc_scr[...] = jnp.zeros_like(acc_scr)
    kpos = lax.broadcasted_iota(jnp.int32, (MLA_TK, MLA_TQ), 0)
    qcol = lax.broadcasted_iota(jnp.int32, (MLA_TK, MLA_TQ), 1)

    def score_chunk(c, diagonal):
        k0 = pl.multiple_of(c * MLA_TK, MLA_TK)
        for hd in range(MLA_HEADS):
            cols = slice(LANES * hd, LANES * (hd + 1))
            st = _dot_nt(k_ref[pl.ds(k0, MLA_TK), cols], q_ref[:, cols])
            if diagonal:
                st = jnp.where(kpos <= qcol, st, _NEG)
            s_scr[c, hd] = st
            m_scr[hd] = jnp.maximum(m_scr[hd], jnp.max(st, axis=0, keepdims=True))

    def score_pass(c, carry):
        score_chunk(c, False)
        return carry

    ones = jnp.ones((MLA_V, MLA_TK), _MX)

    def value_pass(c, carry):
        for hd in range(MLA_HEADS):
            p = jnp.exp2((s_scr[c, hd] - m_scr[hd]) * (scale * math.log2(math.e)))
            v1t = jnp.concatenate([vt_ref[c, MLA_V * hd:MLA_V * (hd + 1), :], ones], axis=0)
            acc_scr[hd] += _dot(v1t, p.astype(_MX))
        return carry

    lax.fori_loop(0, qi, score_pass, 0)
    score_chunk(qi, True)
    lax.fori_loop(0, qi + 1, value_pass, 0)
    ot = jnp.concatenate([acc_scr[hd, :MLA_V] / acc_scr[hd, MLA_V:MLA_V + 1] for hd in range(MLA_HEADS)],
                         axis=0)
    o_ref[...] = ot.T.astype(o_ref.dtype)


def _mla_attn(qm, km, vt, batch, seq):
    n = qm.shape[0]
    nq = seq // MLA_TQ
    wv = MLA_HEADS * MLA_V
    return pl.pallas_call(
        _mla_attn_kernel,
        grid=(batch, nq),
        in_specs=[
            pl.BlockSpec((MLA_TQ, MLA_HEADS * LANES), lambda b, i: (b * nq + i, 0)),
            pl.BlockSpec((seq, MLA_HEADS * LANES), lambda b, i: (b, 0)),
            pl.BlockSpec((seq // MLA_TK, wv, MLA_TK), lambda b, i: (b, 0, 0)),
        ],
        out_specs=pl.BlockSpec((MLA_TQ, wv), lambda b, i: (b * nq + i, 0)),
        out_shape=jax.ShapeDtypeStruct((n, wv), _MX),
        scratch_shapes=[pltpu.VMEM((seq // MLA_TK, MLA_HEADS, MLA_TK, MLA_TQ), _F32),
                        pltpu.VMEM((MLA_HEADS, 1, MLA_TQ), _F32),
                        pltpu.VMEM((MLA_HEADS, 2 * MLA_V, MLA_TQ), _F32)],
        compiler_params=_params(("parallel", "arbitrary")),
        name="mla_attn",
    )(qm, km, vt)


DIL_BLK = 128
DIL_UNROLL = 8


def _dil_attn_kernel(*refs, seq):
    n_grp = len(DIL_PATTERNS)
    qkv = [refs[3 * g:3 * g + 3] for g in range(n_grp)]
    out_ref = refs[3 * n_grp]
    o_slabs = refs[3 * n_grp + 1:3 * n_grp + 1 + n_grp]
    l_slabs = refs[3 * n_grp + 1 + n_grp:]
    scale = DIL_HD ** -0.5
    kidx = lax.broadcasted_iota(jnp.int32, (DIL_BLK, DIL_BLK), 0)
    qidx = lax.broadcasted_iota(jnp.int32, (DIL_BLK, DIL_BLK), 1)
    lane = lax.broadcasted_iota(jnp.int32, (DIL_BLK, LANES), 1)
    dim = lax.broadcasted_iota(jnp.int32, (LANES, DIL_BLK), 0)
    cur_ok = kidx <= qidx

    for g, (window, d) in enumerate(DIL_PATTERNS):
        q_ref, k_ref, v_ref = qkv[g]
        nb = seq // d // DIL_BLK

        def units(it, carry, d=d, nb=nb, q_ref=q_ref, k_ref=k_ref, v_ref=v_ref, g=g):
            staged = []
            for u in range(DIL_UNROLL):
                idx = it * DIL_UNROLL + u
                r = idx // nb
                n = idx % nb
                rows = pl.ds(r + d * DIL_BLK * n, DIL_BLK, stride=d)
                prev = pl.ds(r + d * DIL_BLK * jnp.maximum(n - 1, 0), DIL_BLK, stride=d)
                prev_ok = kidx >= qidx + jnp.where(n > 0, 0, DIL_BLK)
                q2 = q_ref[rows, :] * scale
                kc = k_ref[rows, :].astype(_MX)
                kp = k_ref[prev, :].astype(_MX)
                halves = []
                for half in range(2):
                    mine = (lane < DIL_HD) if half == 0 else (lane >= DIL_HD)
                    qh = jnp.where(mine, q2, 0.0).astype(_MX)
                    halves.append((_dot_nt(kc, qh), _dot_nt(kp, qh)))
                staged.append((rows, prev, prev_ok, halves))
            for rows, prev, prev_ok, halves in staged:
                vct = v_ref[rows, :].T.astype(_MX)
                vpt = v_ref[prev, :].T.astype(_MX)
                ots, lses = [], []
                for sc, sp in halves:
                    sc = jnp.where(cur_ok, sc, _NEG)
                    sp = jnp.where(prev_ok, sp, _NEG)
                    m = jnp.maximum(jnp.max(sc, axis=0, keepdims=True), jnp.max(sp, axis=0, keepdims=True))
                    pc = jnp.exp(sc - m)
                    pp = jnp.exp(sp - m)
                    l = jnp.sum(pc, axis=0, keepdims=True) + jnp.sum(pp, axis=0, keepdims=True)
                    ots.append((_dot(vct, pc.astype(_MX)) + _dot(vpt, pp.astype(_MX))) / l)
                    lses.append(m + jnp.log(l))
                ot = jnp.where(dim < DIL_HD, ots[0], ots[1])
                lt = jnp.where(dim < DIL_HD, lses[0], lses[1])
                o_slabs[g][rows, :] = ot.T
                l_slabs[g][rows, :] = lt.T
            return carry

        lax.fori_loop(0, d * nb // DIL_UNROLL, units, 0)

    ls = [l_slabs[g][...] for g in range(n_grp)]
    m = functools.reduce(jnp.maximum, ls)
    es = [jnp.exp(l - m) for l in ls]
    num = sum(e * o_slabs[g][...] for g, e in enumerate(es))
    out_ref[...] = (num / sum(es)).astype(out_ref.dtype)


def _dil_attn(dil, batch, seq):
    n = dil.shape[0]
    pairs = DIL_HEADS // 2
    spec = lambda g, part: pl.BlockSpec((seq, LANES), lambda b, p: (b, (3 * g + part) * pairs + p))
    in_specs = [spec(g, part) for g in range(len(DIL_PATTERNS)) for part in range(3)]
    return pl.pallas_call(
        functools.partial(_dil_attn_kernel, seq=seq),
        grid=(batch, pairs),
        in_specs=in_specs,
        out_specs=pl.BlockSpec((seq, LANES), lambda b, p: (b, p)),
        out_shape=jax.ShapeDtypeStruct((n, DIL_HEADS * DIL_HD), _MX),
        scratch_shapes=[pltpu.VMEM((seq, LANES), _F32)] * (2 * len(DIL_PATTERNS)),
        compiler_params=_params(("parallel", "parallel")),
        name="dil_attn",
    )(*([dil] * len(in_specs)))


ODD_IN_PAD = 2048


def _odd_proj_kernel(x_ref, g_ref, win_ref, pin_ref, s64_ref, c64_ref, n64_ref,
                     q_ref, kk_ref, vt_ref, qi_ref, ki_ref, wit_ref):
    h = _row_rmsnorm(x_ref[...], g_ref[...]).astype(_MX)
    c64, n64 = c64_ref[...], n64_ref[...]

    def normed(y, c0):
        ss = _seg_sumsq(y, s64_ref)
        return y * lax.rsqrt(ss * pin_ref[0:1, c0:c0 + MXU_WIDTH] + EPS) * pin_ref[1:2, c0:c0 + MXU_WIDTH]

    def rope64(t):
        return _rope_tile(t, SPA_HD // 2, c64, n64)

    nq = SPA_HEADS * SPA_HD
    c_qi = nq + MXU_WIDTH
    c_ki = c_qi + IDX_HEADS * IDX_HD
    product = lambda c0: (lambda: _dot(h, win_ref[:, c0:c0 + MXU_WIDTH]))
    jobs = []
    for j in range(nq // MXU_WIDTH):
        def epi_q(y, c0=MXU_WIDTH * j):
            z = normed(y, c0)
            for t in range(2):
                q_ref[:, c0 + LANES * t:c0 + LANES * (t + 1)] = rope64(z[:, LANES * t:LANES * (t + 1)]).astype(_MX)
        jobs.append((product(MXU_WIDTH * j), epi_q))

    def epi_kv(y):
        z = normed(y, nq)
        kk_ref[...] = rope64(z[:, :LANES]).astype(_MX)
        vt_ref[...] = y[:, LANES:].T.astype(_MX)
    jobs.append((product(nq), epi_kv))

    for j in range(IDX_HEADS * IDX_HD // MXU_WIDTH):
        def epi_qi(y, o0=MXU_WIDTH * j):
            for t in range(2):
                qi_ref[:, o0 + LANES * t:o0 + LANES * (t + 1)] = rope64(y[:, LANES * t:LANES * (t + 1)]).astype(_MX)
        jobs.append((product(c_qi + MXU_WIDTH * j), epi_qi))

    def epi_ki(y):
        ki_ref[...] = rope64(y[:, :LANES]).astype(_MX)
        wi = (y[:, LANES:] * (IDX_HEADS ** -0.5)) * (IDX_HD ** -0.5)
        wit_ref[...] = wi.T[0:IDX_HEADS, :]
    jobs.append((product(c_ki), epi_ki))
    _run_pipelined(jobs)


def _odd_proj(x, g, win, pin, s64, tabs):
    n = x.shape[0]
    tm = PROJ_TM
    row = lambda w: pl.BlockSpec((tm, w), lambda i: (i, 0))
    c64, n64 = tabs
    return pl.pallas_call(
        _odd_proj_kernel,
        grid=(n // tm,),
        in_specs=[row(D_MODEL), _const_spec(g.shape), _const_spec(win.shape), _const_spec(pin.shape),
                  _const_spec(s64.shape), row(LANES), row(LANES)],
        out_specs=[row(SPA_HEADS * SPA_HD), row(LANES),
                   pl.BlockSpec((None, LANES, tm), lambda i: (i, 0, 0)),
                   row(IDX_HEADS * IDX_HD), row(LANES),
                   pl.BlockSpec((IDX_HEADS, tm), lambda i: (0, i))],
        out_shape=[jax.ShapeDtypeStruct((n, SPA_HEADS * SPA_HD), _MX),
                   jax.ShapeDtypeStruct((n, LANES), _MX),
                   jax.ShapeDtypeStruct((n // tm, LANES, tm), _MX),
                   jax.ShapeDtypeStruct((n, IDX_HEADS * IDX_HD), _MX),
                   jax.ShapeDtypeStruct((n, LANES), _MX),
                   jax.ShapeDtypeStruct((IDX_HEADS, n), _F32)],
        compiler_params=_params(("parallel",)),
        name="odd_proj",
    )(x, g, win, pin, s64, c64, n64)


SPA_TQ = 128
SPA_TK = 256
_INT_MIN = -2 ** 31
assert SPA_TK == PROJ_TM and MLA_TK == PROJ_TM and MLA_TQ == MLA_TK


def _sparse_attn_kernel(q_ref, kk_ref, vt_ref, qi_ref, ki_ref, wit_ref, o_ref,
                        key_scr, qs_scr, s_scr, m_scr, acc_scr, *, n_keep):
    qb = pl.program_id(1)
    n_chunks = qb // (SPA_TK // SPA_TQ) + 1
    lane = lax.broadcasted_iota(jnp.int32, (SPA_TQ, LANES), 1)
    left = lane < SPA_HD

    def stack_heads(ref, n_pairs):
        parts = []
        for pr in range(n_pairs):
            t = ref[:, LANES * pr:LANES * (pr + 1)]
            parts.append(jnp.where(left, t, jnp.zeros_like(t)))
            parts.append(jnp.where(left, jnp.zeros_like(t), t))
        return jnp.concatenate(parts, axis=0)

    qi_stack = stack_heads(qi_ref, IDX_HEADS // 2)
    wit = wit_ref[...]
    krow = lax.broadcasted_iota(jnp.int32, (SPA_TK, SPA_TQ), 0)
    qpos = qb * SPA_TQ + lax.broadcasted_iota(jnp.int32, (SPA_TK, SPA_TQ), 1)

    def score_chunk(c, carry):
        k0 = pl.multiple_of(c * SPA_TK, SPA_TK)
        logits = _dot_nt(ki_ref[pl.ds(k0, SPA_TK), :], qi_stack)
        sc = jnp.zeros((SPA_TK, SPA_TQ), _F32)
        for hd in range(IDX_HEADS):
            sc = sc + wit[hd:hd + 1, :] * jnp.maximum(logits[:, SPA_TQ * hd:SPA_TQ * (hd + 1)], 0.0)
        sc = jnp.where((k0 + krow) <= qpos, sc, -jnp.inf)
        bits = pltpu.bitcast(sc, jnp.int32)
        key_scr[c] = bits ^ ((bits >> 31) & jnp.int32(0x7FFFFFFF))
        return carry

    def over_chunks(body, init):
        def two(i, carry):
            return body(2 * i + 1, body(2 * i, carry))
        carry = lax.fori_loop(0, n_chunks // 2, two, init)
        return lax.cond(n_chunks % 2 == 1, lambda cr: body(n_chunks - 1, cr), lambda cr: cr, carry)

    over_chunks(score_chunk, 0)

    def count(pred):
        def body(c, acc):
            hit = pred(key_scr[c], c * SPA_TK)
            return acc + jnp.sum(hit.astype(jnp.int32).reshape(SPA_TK // 8, 8, SPA_TQ), axis=0)
        acc = over_chunks(body, jnp.zeros((8, SPA_TQ), jnp.int32))
        return jnp.sum(acc, axis=0, keepdims=True)

    def refine(t, cand):
        return jnp.where(count(lambda k, k0: k >= cand) >= n_keep, cand, t)

    t = jnp.full((1, SPA_TQ), _INT_MIN, jnp.int32)
    t = refine(t, jnp.zeros((1, SPA_TQ), jnp.int32))
    t = lax.fori_loop(0, 31, lambda i, t: refine(t, t | (jnp.int32(1) << (30 - i))), t)
    n_gt = count(lambda k, k0: k > t)
    n_ge = count(lambda k, k0: k >= t)
    need = n_keep - n_gt
    n_bits = (key_scr.shape[0] * SPA_TK).bit_length()

    def search_last():
        def refine_pos(j, cand):
            below = count(lambda k, k0: (k == t) & ((k0 + krow) < cand))
            return jnp.where(below < need, cand, j)
        return lax.fori_loop(0, n_bits, lambda i, j: refine_pos(j, j | (jnp.int32(1) << (n_bits - 1 - i))),
                             jnp.zeros((1, SPA_TQ), jnp.int32))

    keep_all_ties = lambda: jnp.full((1, SPA_TQ), 2 ** n_bits, jnp.int32)
    last = lax.cond(jnp.max(n_ge) > n_keep, search_last, keep_all_ties)

    qs_scr[...] = stack_heads(q_ref, SPA_HEADS // 2) * (SPA_HD ** -0.5)
    m_scr[...] = jnp.full_like(m_scr, _NEG)
    acc_scr[...] = jnp.zeros_like(acc_scr)
    vrow = lax.broadcasted_iota(jnp.int32, (LANES, SPA_TK), 0)
    pair_w = 2 * SPA_TQ
    pairs = [slice(pair_w * pr, pair_w * (pr + 1)) for pr in range(SPA_HEADS // 2)]

    def score_pass(c, carry):
        k0 = pl.multiple_of(c * SPA_TK, SPA_TK)
        keys = key_scr[c]
        kpos = k0 + krow
        chosen = ((keys > t) | ((keys == t) & (kpos <= last))) & (kpos <= qpos)
        bias = jnp.where(chosen, 0.0, _NEG)
        bias2 = jnp.concatenate([bias, bias], axis=1)
        kk = kk_ref[pl.ds(k0, SPA_TK), :]
        for cols in pairs:
            st = _dot_nt(kk, qs_scr[cols, :]) + bias2
            s_scr[c, :, cols] = st
            m_scr[:, cols] = jnp.maximum(m_scr[:, cols], jnp.max(st, axis=0, keepdims=True))
        return carry

    def value_pass(c, carry):
        vt = vt_ref[c]
        v1t = jnp.where(vrow < SPA_HD, vt, jnp.ones_like(vt))
        for cols in pairs:
            p = jnp.exp(s_scr[c, :, cols] - m_scr[:, cols])
            acc_scr[:, cols] += _dot(v1t, p.astype(_MX))
        return carry

    lax.fori_loop(0, n_chunks, score_pass, 0)
    lax.fori_loop(0, n_chunks, value_pass, 0)
    for pr in range(SPA_HEADS // 2):
        a = acc_scr[:, pair_w * pr:pair_w * pr + SPA_TQ]
        b = acc_scr[:, pair_w * pr + SPA_TQ:pair_w * (pr + 1)]
        tile_t = jnp.concatenate([a[:SPA_HD] / a[SPA_HD:SPA_HD + 1], b[:SPA_HD] / b[SPA_HD:SPA_HD + 1]], axis=0)
        o_ref[:, LANES * pr:LANES * (pr + 1)] = tile_t.T.astype(o_ref.dtype)


def _sparse_attn(q, kk, vt, qi, ki, wit, batch, seq):
    n = q.shape[0]
    nq = seq // SPA_TQ
    n_keep = min(TOPK_MAX, seq // 4)
    return pl.pallas_call(
        functools.partial(_sparse_attn_kernel, n_keep=n_keep),
        grid=(batch, nq),
        in_specs=[
            pl.BlockSpec((SPA_TQ, SPA_HEADS * SPA_HD), lambda b, i: (b * nq + i, 0)),
            pl.BlockSpec((seq, LANES), lambda b, i: (b, 0)),
            pl.BlockSpec((seq // SPA_TK, LANES, SPA_TK), lambda b, i: (b, 0, 0)),
            pl.BlockSpec((SPA_TQ, IDX_HEADS * IDX_HD), lambda b, i: (b * nq + i, 0)),
            pl.BlockSpec((seq, LANES), lambda b, i: (b, 0)),
            pl.BlockSpec((IDX_HEADS, SPA_TQ), lambda b, i: (0, b * nq + i)),
        ],
        out_specs=pl.BlockSpec((SPA_TQ, SPA_HEADS * SPA_HD), lambda b, i: (b * nq + i, 0)),
        out_shape=jax.ShapeDtypeStruct((n, SPA_HEADS * SPA_HD), _MX),
        scratch_shapes=[pltpu.VMEM((seq // SPA_TK, SPA_TK, SPA_TQ), jnp.int32),
                        pltpu.VMEM((SPA_HEADS * SPA_TQ, LANES), _MX),
                        pltpu.VMEM((seq // SPA_TK, SPA_TK, SPA_HEADS * SPA_TQ), _F32),
                        pltpu.VMEM((1, SPA_HEADS * SPA_TQ), _F32),
                        pltpu.VMEM((LANES, SPA_HEADS * SPA_TQ), _F32)],
        compiler_params=_params(("parallel", "arbitrary")),
        name="sparse_attn",
    )(q, kk, vt, qi, ki, wit)


def _mem_kv_kernel(mem_ref, g_ref, w_ref, kg_ref, k_ref, v_ref):
    h = _row_rmsnorm(mem_ref[...], g_ref[...]).astype(_MX)
    kg = kg_ref[...]
    for hd in range(X_HEADS):
        cols = slice(X_HD * hd, X_HD * (hd + 1))
        y = _dot(h, w_ref[:, cols])
        k_ref[:, cols] = _row_rmsnorm(y, kg).astype(_MX)
    for hd in range(X_HEADS):
        cols = slice(X_HD * hd, X_HD * (hd + 1))
        v_ref[:, cols] = _dot(h, w_ref[:, D_MODEL + X_HD * hd:D_MODEL + X_HD * (hd + 1)]).astype(_MX)


def _mem_kv(mem, g, wkv, kg):
    n, m = mem.shape[0], 256
    row = pl.BlockSpec((m, D_MODEL), lambda i: (i, 0))
    return pl.pallas_call(
        _mem_kv_kernel,
        grid=(n // m,),
        in_specs=[row, _const_spec(g.shape), _const_spec(wkv.shape), _const_spec(kg.shape)],
        out_specs=[row, row],
        out_shape=[jax.ShapeDtypeStruct((n, D_MODEL), _MX)] * 2,
        compiler_params=_params(("parallel",)),
        name="mem_kv",
    )(mem, g, wkv, kg)


XATT_TM = 512


def _xattn_kernel(*refs, n_mix):
    x_ref = refs[0]
    mix_refs = refs[1:1 + n_mix]
    mixw_refs = refs[1 + n_mix:1 + 2 * n_mix]
    g_ref, wq_ref, qg_ref, k_ref, v_ref, wo_ref, out_ref, o_scr = refs[1 + 2 * n_mix:]
    x = x_ref[...]
    for a_ref, w_ref in zip(mix_refs, mixw_refs):
        x = x + _dot(a_ref[...], w_ref[...])
    h = _row_rmsnorm(x, g_ref[...]).astype(_MX)
    qg = qg_ref[...]
    scale = X_HD ** -0.5
    heads = [slice(X_HD * hd, X_HD * (hd + 1)) for hd in range(X_HEADS)]
    qs = [_dot(h, wq_ref[:, cols]) for cols in heads]
    qs = [_row_rmsnorm(q, qg).astype(_MX) for q in qs]
    ss = [_dot_nt(q, k_ref[:, cols]) * scale for q, cols in zip(qs, heads)]
    for s, cols in zip(ss, heads):
        p = jnp.exp(s - jnp.max(s, axis=-1, keepdims=True))
        p = p / jnp.sum(p, axis=-1, keepdims=True)
        o_scr[:, cols] = _dot(p.astype(_MX), v_ref[:, cols]).astype(_MX)
    out_ref[...] = x + _dot(o_scr[...], wo_ref[...])


def _xattn(x, mix, mix_w, g, wq, qg, k, v, wo, batch, seq, mem_len):
    n = x.shape[0]
    tm = XATT_TM
    nt = seq // tm
    row = lambda w: pl.BlockSpec((tm, w), lambda b, i: (b * nt + i, 0))
    const = lambda a: pl.BlockSpec(a.shape, lambda b, i: (0,) * a.ndim)
    memspec = pl.BlockSpec((mem_len, D_MODEL), lambda b, i: (b, 0))
    return pl.pallas_call(
        functools.partial(_xattn_kernel, n_mix=len(mix)),
        grid=(batch, nt),
        in_specs=([row(D_MODEL)] + [row(a.shape[1]) for a in mix] + [const(w) for w in mix_w]
                  + [const(g), const(wq), const(qg), memspec, memspec, const(wo)]),
        out_specs=row(D_MODEL),
        out_shape=jax.ShapeDtypeStruct(x.shape, _F32),
        scratch_shapes=[pltpu.VMEM((tm, D_MODEL), _MX)],
        compiler_params=_params(("parallel", "parallel")),
        name="xattn",
    )(x, *mix, *mix_w, g, wq, qg, k, v, wo)


def _seg_matrix(seg):
    idx = np.arange(MXU_WIDTH) // seg
    return jnp.asarray(idx[:, None] == idx[None, :], _MX)


def _rope_tables(positions, dim):
    inv = jnp.exp(-math.log(ROPE_THETA) * jnp.arange(0, dim, 2, dtype=_F32) / dim)
    ang = positions.astype(_F32).reshape(-1)[:, None] * inv
    return jnp.cos(ang), jnp.sin(ang)


def _tables(positions):
    cos64, sin64 = _rope_tables(positions, DIL_HD)
    cos32, sin32 = _rope_tables(positions, MLA_ROPE)
    n = cos64.shape[0]
    c64 = jnp.tile(cos64, (1, 4))
    n64 = jnp.tile(jnp.concatenate([-sin64, sin64], axis=1), (1, 2))
    ones, zeros = jnp.ones((n, MLA_NOPE), _F32), jnp.zeros((n, MLA_NOPE), _F32)
    pad1, pad0 = jnp.ones((n, 32), _F32), jnp.zeros((n, 32), _F32)
    cm = jnp.concatenate([ones, cos32, cos32, pad1], axis=1)
    nm = jnp.concatenate([zeros, -sin32, sin32, pad0], axis=1)
    return c64, n64, cm, nm


def _even_weights(w_in, q_lora_g, kv_lora_g, w_uq, w_ukv, q_g, k_g, dq_g, dk_g, w_out):
    o1, o2, o3 = MLA_Q_LORA, MLA_Q_LORA + MLA_KV_LORA, MLA_Q_LORA + MLA_KV_LORA + MLA_ROPE
    d = w_in.shape[0]
    z = lambda w: jnp.zeros((d, w), _F32)
    win = jnp.concatenate([w_in[:, :o2], z(64), w_in[:, o2:o3], z(32), w_in[:, o3:]], axis=1).astype(_MX)
    ones = lambda w: jnp.ones((w,), _F32)
    zeros = lambda w: jnp.zeros((w,), _F32)
    grp_gain = jnp.concatenate([jnp.tile(dq_g, DIL_HEADS), jnp.tile(dk_g, DIL_HEADS), ones(256)])
    pin = jnp.stack([
        jnp.concatenate([ones(256) / 256, ones(128) / 128, ones(128) / MLA_ROPE, ones(DIL_COLS) / DIL_HD]),
        jnp.concatenate([q_lora_g, kv_lora_g, zeros(64), k_g[MLA_NOPE:], zeros(32), jnp.tile(grp_gain, 3)]),
    ])
    wq = jnp.pad(w_uq.reshape(MLA_Q_LORA, MLA_HEADS, MLA_NOPE + MLA_ROPE), ((0, 0), (0, 0), (0, 32)))
    wq = wq.reshape(MLA_Q_LORA, MLA_HEADS * LANES).astype(_MX)
    pq = jnp.stack([
        jnp.tile(jnp.concatenate([ones(64) / MLA_NOPE, ones(64) / MLA_ROPE]), MLA_HEADS),
        jnp.tile(jnp.concatenate([q_g, zeros(32)]), MLA_HEADS),
    ])
    ukv = w_ukv.reshape(MLA_KV_LORA, MLA_HEADS, MLA_NOPE + MLA_V)
    wk = jnp.pad(ukv[:, :, :MLA_NOPE], ((0, 0), (0, 0), (0, 64))).reshape(MLA_KV_LORA, MLA_HEADS * LANES).astype(_MX)
    wv = ukv[:, :, MLA_NOPE:].reshape(MLA_KV_LORA, MLA_HEADS * MLA_V).astype(_MX)
    pk = jnp.stack([
        jnp.tile(ones(LANES) / MLA_NOPE, MLA_HEADS),
        jnp.tile(jnp.concatenate([k_g[:MLA_NOPE], zeros(64)]), MLA_HEADS),
    ])
    return win, wq, wk, wv, pin, pq, pk, w_out.astype(_MX)


def _odd_weights(w_in, q_g, k_g, w_out):
    o1 = SPA_HEADS * SPA_HD
    o2, o3 = o1 + SPA_HD, o1 + 2 * SPA_HD
    o4 = o3 + IDX_HEADS * IDX_HD
    o5 = o4 + IDX_HD
    d = w_in.shape[0]
    k, v, ki = w_in[:, o1:o2], w_in[:, o2:o3], w_in[:, o4:o5]
    win = jnp.concatenate([w_in[:, :o1], k, k, v, v, w_in[:, o3:o4], ki, ki, w_in[:, o5:],
                           jnp.zeros((d, LANES - IDX_HEADS), _F32)], axis=1).astype(_MX)
    ones = jnp.ones((ODD_IN_PAD,), _F32)
    gain = jnp.concatenate([jnp.tile(q_g, SPA_HEADS), k_g, k_g, jnp.ones((ODD_IN_PAD - o1 - 2 * SPA_HD,), _F32)])
    pin = jnp.stack([ones / SPA_HD, gain])
    return win, pin, w_out.astype(_MX)


def kernel(x, mem, positions, ffn1_norm, ffn1_w13, ffn1_w2, mix_norm, xattn_norm, mem_norm,
           xattn_wq, xattn_wkv, xattn_q_gain, xattn_k_gain, xattn_wo, ffn2_norm, ffn2_w13, ffn2_w2,
           even_w_in, mla_q_lora_norm, mla_kv_lora_norm, mla_w_uq, mla_w_ukv, mla_q_gain, mla_k_gain,
           dil_q_gain, dil_k_gain, even_w_out, odd_w_in, sparse_q_gain, sparse_k_gain, odd_w_out):
    batch, seq, d = x.shape
    mem_len = mem.shape[1]
    n = batch * seq
    c64, n64, cm, nm = _tables(positions)
    s64, s128, s256 = _seg_matrix(64), _seg_matrix(128), _seg_matrix(256)
    bf = lambda a: a.astype(_MX)
    row3 = lambda a: a.reshape(a.shape[0], 1, a.shape[1])
    f1w13, f1w2, f2w13, f2w2 = bf(ffn1_w13), bf(ffn1_w2), bf(ffn2_w13), bf(ffn2_w2)
    f1g, f2g = row3(ffn1_norm), row3(ffn2_norm)
    xwq, xwkv, xwo = bf(xattn_wq), bf(xattn_wkv), bf(xattn_wo)

    xs = x.reshape(n, d)
    mems = mem.reshape(batch * mem_len, d)
    for i in range(DEPTH):
        j = i // 2
        xs = _ffn(xs, f1g, f1w13, f1w2, i)
        g_mix = mix_norm[i][None, :]
        if i % 2 == 0:
            win, wq, wk, wv, pin, pq, pk, wout = _even_weights(
                even_w_in[j], mla_q_lora_norm[j], mla_kv_lora_norm[j], mla_w_uq[j], mla_w_ukv[j],
                mla_q_gain[j], mla_k_gain[j], dil_q_gain[j], dil_k_gain[j], even_w_out[j])
            qm, km, vt, dil = _even_proj(xs, g_mix, win, wq, wk, wv, pin, pq, pk,
                                         (s64, s128, s256), (c64, n64, cm, nm))
            n_mla = MLA_HEADS * MLA_V
            mix = [_mla_attn(qm, km, vt, batch, seq), _dil_attn(dil, batch, seq)]
            mix_w = [wout[:n_mla], wout[n_mla:]]
        else:
            win, pin, wout = _odd_weights(odd_w_in[j], sparse_q_gain[j], sparse_k_gain[j], odd_w_out[j])
            q, kk, vt, qi, ki, wit = _odd_proj(xs, g_mix, win, pin, s64, (c64, n64))
            mix = [_sparse_attn(q, kk, vt, qi, ki, wit, batch, seq)]
            mix_w = [wout]
        mk, mv = _mem_kv(mems, mem_norm[i][None, :], xwkv[i], xattn_k_gain[i][None, :])
        xs = _xattn(xs, mix, mix_w, xattn_norm[i][None, :], xwq[i], xattn_q_gain[i][None, :], mk, mv, xwo[i],
                    batch, seq, mem_len)
        xs = _ffn(xs, f2g, f2w13, f2w2, i)
    return xs.reshape(batch, seq, d)
```

```python
import functools
import math

import jax
import jax.numpy as jnp
import numpy as np
from jax import lax
from jax.experimental import pallas as pl
from jax.experimental.pallas import tpu as pltpu

D_MODEL = 1024
DEPTH = 4
EPS = 1e-6
D_FF = 2816
ROPE_THETA = 10000.0
MLA_HEADS, MLA_NOPE, MLA_ROPE, MLA_V = 8, 64, 32, 64
MLA_Q_LORA, MLA_KV_LORA = 256, 128
DIL_PATTERNS = ((128, 1), (512, 4), (2048, 16))
DIL_HEADS, DIL_HD = 4, 64
SPA_HEADS, SPA_HD = 16, 64
IDX_HEADS, IDX_HD = 8, 64
TOPK_MAX = 256
X_HEADS = 4
X_HD = D_MODEL // X_HEADS

LANES = 128
MXU_WIDTH = 256
VMEM_LIMIT_BYTES = 56 * 1024 * 1024

_MX = jnp.bfloat16
_F32 = jnp.float32
_NEG = -1e30


def _dot(a, b):
    return jnp.dot(a, b, preferred_element_type=_F32)


def _dot_nt(a, b):
    return lax.dot_general(a, b, (((1,), (1,)), ((), ())), preferred_element_type=_F32)


def _params(sem, flags=None):
    return pltpu.CompilerParams(dimension_semantics=sem, vmem_limit_bytes=VMEM_LIMIT_BYTES, flags=flags)


def _row_rmsnorm(x, g):
    return x * lax.rsqrt(jnp.mean(x * x, axis=-1, keepdims=True) + EPS) * g


def _seg_sumsq(y, seg_ref):
    y2 = y * y
    hi = y2.astype(_MX)
    lo = (y2 - hi.astype(_F32)).astype(_MX)
    s = seg_ref[...]
    return _dot(hi, s) + _dot(lo, s)


def _swap_lanes(z, d):
    lane = lax.broadcasted_iota(jnp.int32, z.shape, 1)
    fwd = pltpu.roll(z, LANES - d, 1)
    bwd = pltpu.roll(z, d, 1)
    return jnp.where((lane & d) == 0, fwd, bwd)


def _rope_tile(z, d, cos, sin_signed):
    return z * cos + _swap_lanes(z, d) * sin_signed


def _run_pipelined(jobs):
    y_next = jobs[0][0]()
    for i, (_, epilogue) in enumerate(jobs):
        y = y_next
        if i + 1 < len(jobs):
            y_next = jobs[i + 1][0]()
        epilogue(y)


FFN_TM = 512
FFN_CHUNKS = (768, 768, 768, 512)
assert sum(FFN_CHUNKS) == D_FF and all(c % MXU_WIDTH == 0 for c in FFN_CHUNKS)


def _ffn_kernel(x_ref, g_ref, w13_ref, w2_ref, o_ref, act_scr):
    x = x_ref[...]
    h = _row_rmsnorm(x, g_ref[...]).astype(_MX)
    c0 = 0
    for width in FFN_CHUNKS:
        gate = _dot(h, w13_ref[:, c0:c0 + width])
        up = _dot(h, w13_ref[:, D_FF + c0:D_FF + c0 + width])
        act_scr[:, c0:c0 + width] = (jax.nn.silu(gate) * up).astype(_MX)
        c0 += width
    o_ref[...] = x + 0.5 * _dot(act_scr[...], w2_ref[...])


def _ffn(x, g, w13, w2, layer):
    n = x.shape[0]
    resident = pl.Buffered(1)
    return pl.pallas_call(
        _ffn_kernel,
        grid=(n // FFN_TM,),
        in_specs=[
            pl.BlockSpec((FFN_TM, D_MODEL), lambda i: (i, 0)),
            pl.BlockSpec((None, 1, D_MODEL), lambda i: (layer, 0, 0)),
            pl.BlockSpec((None, D_MODEL, 2 * D_FF), lambda i: (layer, 0, 0), pipeline_mode=resident),
            pl.BlockSpec((None, D_FF, D_MODEL), lambda i: (layer, 0, 0), pipeline_mode=resident),
        ],
        out_specs=pl.BlockSpec((FFN_TM, D_MODEL), lambda i: (i, 0)),
        out_shape=jax.ShapeDtypeStruct(x.shape, _F32),
        scratch_shapes=[pltpu.VMEM((FFN_TM, D_FF), _MX)],
        compiler_params=_params(("parallel",)),
        name="ffn",
    )(x, g, w13, w2)


PROJ_TM = 256
EVEN_IN_PAD = 2816
DIL_COLS = 2304


def _even_proj_kernel(x_ref, g_ref, win_ref, wq_ref, wk_ref, wv_ref, pin_ref, pq_ref, pk_ref,
                      s64_ref, s128_ref, s256_ref, c64_ref, n64_ref, cm_ref, nm_ref,
                      qm_ref, km_ref, vt_ref, dil_ref):
    h = _row_rmsnorm(x_ref[...], g_ref[...]).astype(_MX)
    c64, n64, cm, nm = c64_ref[...], n64_ref[...], cm_ref[...], nm_ref[...]

    def normed(y, seg_ref, p_ref, c0):
        ss = _seg_sumsq(y, seg_ref)
        return y * lax.rsqrt(ss * p_ref[0:1, c0:c0 + MXU_WIDTH] + EPS) * p_ref[1:2, c0:c0 + MXU_WIDTH]

    tiles = lambda c0: [(t, slice(c0 + LANES * t, c0 + LANES * (t + 1))) for t in range(2)]
    latent = {}
    jobs = []

    def epi_cq(y):
        latent["cq"] = normed(y, s256_ref, pin_ref, 0).astype(_MX)
    jobs.append((lambda: _dot(h, win_ref[:, 0:256]), epi_cq))

    def epi_ckv(y):
        z = normed(y, s128_ref, pin_ref, 256)
        latent["ckv"] = z[:, :LANES].astype(_MX)
        latent["k_rope"] = _rope_tile(z[:, LANES:], MLA_ROPE // 2, cm, nm)
    jobs.append((lambda: _dot(h, win_ref[:, 256:512]), epi_ckv))

    for j in range(DIL_COLS // MXU_WIDTH):
        def epi_dil(y, j=j):
            if j % 3 < 2:
                z = normed(y, s64_ref, pin_ref, 512 + MXU_WIDTH * j)
                for t, cols in tiles(MXU_WIDTH * j):
                    dil_ref[:, cols] = _rope_tile(z[:, LANES * t:LANES * (t + 1)], DIL_HD // 2, c64, n64)
            else:
                dil_ref[:, MXU_WIDTH * j:MXU_WIDTH * (j + 1)] = y
        jobs.append((lambda j=j: _dot(h, win_ref[:, 512 + MXU_WIDTH * j:512 + MXU_WIDTH * (j + 1)]), epi_dil))

    for j in range(MLA_HEADS * LANES // MXU_WIDTH):
        c0 = MXU_WIDTH * j
        def epi_q(y, c0=c0):
            z = normed(y, s64_ref, pq_ref, c0)
            for t, cols in tiles(c0):
                qm_ref[:, cols] = _rope_tile(z[:, LANES * t:LANES * (t + 1)], MLA_ROPE // 2, cm, nm).astype(_MX)
        jobs.append((lambda c0=c0: _dot(latent["cq"], wq_ref[:, c0:c0 + MXU_WIDTH]), epi_q))

        def epi_k(y, c0=c0):
            z = normed(y, s64_ref, pk_ref, c0)
            for t, cols in tiles(c0):
                km_ref[:, cols] = (z[:, LANES * t:LANES * (t + 1)] + latent["k_rope"]).astype(_MX)
        jobs.append((lambda c0=c0: _dot(latent["ckv"], wk_ref[:, c0:c0 + MXU_WIDTH]), epi_k))

    for j in range(MLA_HEADS * MLA_V // MXU_WIDTH):
        c0 = MXU_WIDTH * j
        def epi_v(y, c0=c0):
            for t, cols in tiles(c0):
                vt_ref[cols, :] = y[:, LANES * t:LANES * (t + 1)].T.astype(_MX)
        jobs.append((lambda c0=c0: _dot(latent["ckv"], wv_ref[:, c0:c0 + MXU_WIDTH]), epi_v))
    _run_pipelined(jobs)


def _const_spec(shape):
    nd = len(shape)
    return pl.BlockSpec(shape, lambda i: (0,) * nd)


def _even_proj(x, g, win, wq, wk, wv, pin, pq, pk, segs, tabs):
    n = x.shape[0]
    tm = PROJ_TM
    row = lambda w: pl.BlockSpec((tm, w), lambda i: (i, 0))
    s64, s128, s256 = segs
    c64, n64, cm, nm = tabs
    return pl.pallas_call(
        _even_proj_kernel,
        grid=(n // tm,),
        in_specs=[row(D_MODEL), _const_spec(g.shape), _const_spec(win.shape), _const_spec(wq.shape),
                  _const_spec(wk.shape), _const_spec(wv.shape), _const_spec(pin.shape),
                  _const_spec(pq.shape), _const_spec(pk.shape), _const_spec(s64.shape),
                  _const_spec(s128.shape), _const_spec(s256.shape),
                  row(LANES), row(LANES), row(LANES), row(LANES)],
        out_specs=[row(MLA_HEADS * LANES), row(MLA_HEADS * LANES),
                   pl.BlockSpec((None, MLA_HEADS * MLA_V, tm), lambda i: (i, 0, 0)), row(DIL_COLS)],
        out_shape=[jax.ShapeDtypeStruct((n, MLA_HEADS * LANES), _MX),
                   jax.ShapeDtypeStruct((n, MLA_HEADS * LANES), _MX),
                   jax.ShapeDtypeStruct((n // tm, MLA_HEADS * MLA_V, tm), _MX),
                   jax.ShapeDtypeStruct((n, DIL_COLS), _F32)],
        compiler_params=_params(("parallel",)),
        name="even_proj",
    )(x, g, win, wq, wk, wv, pin, pq, pk, s64, s128, s256, c64, n64, cm, nm)


MLA_TQ = 256
MLA_TK = 256


def _mla_attn_kernel(q_ref, k_ref, vt_ref, o_ref, s_scr, m_scr, acc_scr):
    qi = pl.program_id(1)
    scale = (MLA_NOPE + MLA_ROPE) ** -0.5
    m_scr[...] = jnp.full_like(m_scr, _NEG)
    acc_scr[...] = jnp.zeros_like(acc_scr)
    kpos = lax.broadcasted_iota(jnp.int32, (MLA_TK, MLA_TQ), 0)
    qcol = lax.broadcasted_iota(jnp.int32, (MLA_TK, MLA_TQ), 1)

    def score_chunk(c, diagonal):
        k0 = pl.multiple_of(c * MLA_TK, MLA_TK)
        for hd in range(MLA_HEADS):
            cols = slice(LANES * hd, LANES * (hd + 1))
            st = _dot_nt(k_ref[pl.ds(k0, MLA_TK), cols], q_ref[:, cols])
            if diagonal:
                st = jnp.where(kpos <= qcol, st, _NEG)
            s_scr[c, hd] = st
            m_scr[hd] = jnp.maximum(m_scr[hd], jnp.max(st, axis=0, keepdims=True))

    def score_pass(c, carry):
        score_chunk(c, False)
        return carry

    ones = jnp.ones((MLA_V, MLA_TK), _MX)

    def value_pass(c, carry):
        for hd in range(MLA_HEADS):
            p = jnp.exp2((s_scr[c, hd] - m_scr[hd]) * (scale * math.log2(math.e)))
            v1t = jnp.concatenate([vt_ref[c, MLA_V * hd:MLA_V * (hd + 1), :], ones], axis=0)
            acc_scr[hd] += _dot(v1t, p.astype(_MX))
        return carry

    lax.fori_loop(0, qi, score_pass, 0)
    score_chunk(qi, True)
    lax.fori_loop(0, qi + 1, value_pass, 0)
    ot = jnp.concatenate([acc_scr[hd, :MLA_V] / acc_scr[hd, MLA_V:MLA_V + 1] for hd in range(MLA_HEADS)],
                         axis=0)
    o_ref[...] = ot.T.astype(o_ref.dtype)


def _mla_attn(qm, km, vt, batch, seq):
    n = qm.shape[0]
    nq = seq // MLA_TQ
    wv = MLA_HEADS * MLA_V
    return pl.pallas_call(
        _mla_attn_kernel,
        grid=(batch, nq),
        in_specs=[
            pl.BlockSpec((MLA_TQ, MLA_HEADS * LANES), lambda b, i: (b * nq + i, 0)),
            pl.BlockSpec((seq, MLA_HEADS * LANES), lambda b, i: (b, 0)),
            pl.BlockSpec((seq // MLA_TK, wv, MLA_TK), lambda b, i: (b, 0, 0)),
        ],
        out_specs=pl.BlockSpec((MLA_TQ, wv), lambda b, i: (b * nq + i, 0)),
        out_shape=jax.ShapeDtypeStruct((n, wv), _MX),
        scratch_shapes=[pltpu.VMEM((seq // MLA_TK, MLA_HEADS, MLA_TK, MLA_TQ), _F32),
                        pltpu.VMEM((MLA_HEADS, 1, MLA_TQ), _F32),
                        pltpu.VMEM((MLA_HEADS, 2 * MLA_V, MLA_TQ), _F32)],
        compiler_params=_params(("parallel", "arbitrary")),
        name="mla_attn",
    )(qm, km, vt)


DIL_BLK = 128
DIL_UNROLL = 8


def _dil_attn_kernel(*refs, seq):
    n_grp = len(DIL_PATTERNS)
    qkv = [refs[3 * g:3 * g + 3] for g in range(n_grp)]
    out_ref = refs[3 * n_grp]
    o_slabs = refs[3 * n_grp + 1:3 * n_grp + 1 + n_grp]
    l_slabs = refs[3 * n_grp + 1 + n_grp:]
    scale = DIL_HD ** -0.5
    kidx = lax.broadcasted_iota(jnp.int32, (DIL_BLK, DIL_BLK), 0)
    qidx = lax.broadcasted_iota(jnp.int32, (DIL_BLK, DIL_BLK), 1)
    lane = lax.broadcasted_iota(jnp.int32, (DIL_BLK, LANES), 1)
    dim = lax.broadcasted_iota(jnp.int32, (LANES, DIL_BLK), 0)
    cur_ok = kidx <= qidx

    for g, (window, d) in enumerate(DIL_PATTERNS):
        q_ref, k_ref, v_ref = qkv[g]
        nb = seq // d // DIL_BLK

        def units(it, carry, d=d, nb=nb, q_ref=q_ref, k_ref=k_ref, v_ref=v_ref, g=g):
            staged = []
            for u in range(DIL_UNROLL):
                idx = it * DIL_UNROLL + u
                r = idx // nb
                n = idx % nb
                rows = pl.ds(r + d * DIL_BLK * n, DIL_BLK, stride=d)
                prev = pl.ds(r + d * DIL_BLK * jnp.maximum(n - 1, 0), DIL_BLK, stride=d)
                prev_ok = kidx >= qidx + jnp.where(n > 0, 0, DIL_BLK)
                q2 = q_ref[rows, :] * scale
                kc = k_ref[rows, :].astype(_MX)
                kp = k_ref[prev, :].astype(_MX)
                halves = []
                for half in range(2):
                    mine = (lane < DIL_HD) if half == 0 else (lane >= DIL_HD)
                    qh = jnp.where(mine, q2, 0.0).astype(_MX)
                    halves.append((_dot_nt(kc, qh), _dot_nt(kp, qh)))
                staged.append((rows, prev, prev_ok, halves))
            for rows, prev, prev_ok, halves in staged:
                vct = v_ref[rows, :].T.astype(_MX)
                vpt = v_ref[prev, :].T.astype(_MX)
                ots, lses = [], []
                for sc, sp in halves:
                    sc = jnp.where(cur_ok, sc, _NEG)
                    sp = jnp.where(prev_ok, sp, _NEG)
                    m = jnp.maximum(jnp.max(sc, axis=0, keepdims=True), jnp.max(sp, axis=0, keepdims=True))
                    pc = jnp.exp(sc - m)
                    pp = jnp.exp(sp - m)
                    l = jnp.sum(pc, axis=0, keepdims=True) + jnp.sum(pp, axis=0, keepdims=True)
                    ots.append((_dot(vct, pc.astype(_MX)) + _dot(vpt, pp.astype(_MX))) / l)
                    lses.append(m + jnp.log(l))
                ot = jnp.where(dim < DIL_HD, ots[0], ots[1])
                lt = jnp.where(dim < DIL_HD, lses[0], lses[1])
                o_slabs[g][rows, :] = ot.T
                l_slabs[g][rows, :] = lt.T
            return carry

        lax.fori_loop(0, d * nb // DIL_UNROLL, units, 0)

    ls = [l_slabs[g][...] for g in range(n_grp)]
    m = functools.reduce(jnp.maximum, ls)
    es = [jnp.exp(l - m) for l in ls]
    num = sum(e * o_slabs[g][...] for g, e in enumerate(es))
    out_ref[...] = (num / sum(es)).astype(out_ref.dtype)


def _dil_attn(dil, batch, seq):
    n = dil.shape[0]
    pairs = DIL_HEADS // 2
    spec = lambda g, part: pl.BlockSpec((seq, LANES), lambda b, p: (b, (3 * g + part) * pairs + p))
    in_specs = [spec(g, part) for g in range(len(DIL_PATTERNS)) for part in range(3)]
    return pl.pallas_call(
        functools.partial(_dil_attn_kernel, seq=seq),
        grid=(batch, pairs),
        in_specs=in_specs,
        out_specs=pl.BlockSpec((seq, LANES), lambda b, p: (b, p)),
        out_shape=jax.ShapeDtypeStruct((n, DIL_HEADS * DIL_HD), _MX),
        scratch_shapes=[pltpu.VMEM((seq, LANES), _F32)] * (2 * len(DIL_PATTERNS)),
        compiler_params=_params(("parallel", "parallel")),
        name="dil_attn",
    )(*([dil] * len(in_specs)))


ODD_IN_PAD = 2048


def _odd_proj_kernel(x_ref, g_ref, win_ref, pin_ref, s64_ref, c64_ref, n64_ref,
                     q_ref, kk_ref, vt_ref, qi_ref, ki_ref, wit_ref):
    h = _row_rmsnorm(x_ref[...], g_ref[...]).astype(_MX)
    c64, n64 = c64_ref[...], n64_ref[...]

    def normed(y, c0):
        ss = _seg_sumsq(y, s64_ref)
        return y * lax.rsqrt(ss * pin_ref[0:1, c0:c0 + MXU_WIDTH] + EPS) * pin_ref[1:2, c0:c0 + MXU_WIDTH]

    def rope64(t):
        return _rope_tile(t, SPA_HD // 2, c64, n64)

    nq = SPA_HEADS * SPA_HD
    c_qi = nq + MXU_WIDTH
    c_ki = c_qi + IDX_HEADS * IDX_HD
    product = lambda c0: (lambda: _dot(h, win_ref[:, c0:c0 + MXU_WIDTH]))
    jobs = []
    for j in range(nq // MXU_WIDTH):
        def epi_q(y, c0=MXU_WIDTH * j):
            z = normed(y, c0)
            for t in range(2):
                q_ref[:, c0 + LANES * t:c0 + LANES * (t + 1)] = rope64(z[:, LANES * t:LANES * (t + 1)]).astype(_MX)
        jobs.append((product(MXU_WIDTH * j), epi_q))

    def epi_kv(y):
        z = normed(y, nq)
        kk_ref[...] = rope64(z[:, :LANES]).astype(_MX)
        vt_ref[...] = y[:, LANES:].T.astype(_MX)
    jobs.append((product(nq), epi_kv))

    for j in range(IDX_HEADS * IDX_HD // MXU_WIDTH):
        def epi_qi(y, o0=MXU_WIDTH * j):
            for t in range(2):
                qi_ref[:, o0 + LANES * t:o0 + LANES * (t + 1)] = rope64(y[:, LANES * t:LANES * (t + 1)]).astype(_MX)
        jobs.append((product(c_qi + MXU_WIDTH * j), epi_qi))

    def epi_ki(y):
        ki_ref[...] = rope64(y[:, :LANES]).astype(_MX)
        wi = (y[:, LANES:] * (IDX_HEADS ** -0.5)) * (IDX_HD ** -0.5)
        wit_ref[...] = wi.T[0:IDX_HEADS, :]
    jobs.append((product(c_ki), epi_ki))
    _run_pipelined(jobs)


def _odd_proj(x, g, win, pin, s64, tabs):
    n = x.shape[0]
    tm = PROJ_TM
    row = lambda w: pl.BlockSpec((tm, w), lambda i: (i, 0))
    c64, n64 = tabs
    return pl.pallas_call(
        _odd_proj_kernel,
        grid=(n // tm,),
        in_specs=[row(D_MODEL), _const_spec(g.shape), _const_spec(win.shape), _const_spec(pin.shape),
                  _const_spec(s64.shape), row(LANES), row(LANES)],
        out_specs=[row(SPA_HEADS * SPA_HD), row(LANES),
                   pl.BlockSpec((None, LANES, tm), lambda i: (i, 0, 0)),
                   row(IDX_HEADS * IDX_HD), row(LANES),
                   pl.BlockSpec((IDX_HEADS, tm), lambda i: (0, i))],
        out_shape=[jax.ShapeDtypeStruct((n, SPA_HEADS * SPA_HD), _MX),
                   jax.ShapeDtypeStruct((n, LANES), _MX),
                   jax.ShapeDtypeStruct((n // tm, LANES, tm), _MX),
                   jax.ShapeDtypeStruct((n, IDX_HEADS * IDX_HD), _MX),
                   jax.ShapeDtypeStruct((n, LANES), _MX),
                   jax.ShapeDtypeStruct((IDX_HEADS, n), _F32)],
        compiler_params=_params(("parallel",)),
        name="odd_proj",
    )(x, g, win, pin, s64, c64, n64)


SPA_TQ = 128
SPA_TK = 256
_INT_MIN = -2 ** 31
assert SPA_TK == PROJ_TM and MLA_TK == PROJ_TM and MLA_TQ == MLA_TK


_WORD_BITS = 32
assert SPA_TK == 8 * _WORD_BITS


def _bit_transpose32(words):
    words = list(words)
    for shift, mask in ((16, 0x0000FFFF), (8, 0x00FF00FF), (4, 0x0F0F0F0F), (2, 0x33333333), (1, 0x55555555)):
        for k in range(_WORD_BITS):
            if k & shift == 0:
                t = (lax.shift_right_logical(words[k], jnp.int32(shift)) ^ words[k + shift]) & jnp.int32(mask)
                words[k] = words[k] ^ (t << shift)
                words[k + shift] = words[k + shift] ^ t
    return words


def _sparse_attn_kernel(q_ref, kk_ref, vt_ref, qi_ref, ki_ref, wit_ref, o_ref,
                        key_scr, plane_scr, qs_scr, s_scr, m_scr, acc_scr, *, n_keep):
    qb = pl.program_id(1)
    n_chunks = qb // (SPA_TK // SPA_TQ) + 1
    lane = lax.broadcasted_iota(jnp.int32, (SPA_TQ, LANES), 1)
    left = lane < SPA_HD

    def stack_heads(ref, n_pairs):
        parts = []
        for pr in range(n_pairs):
            t = ref[:, LANES * pr:LANES * (pr + 1)]
            parts.append(jnp.where(left, t, jnp.zeros_like(t)))
            parts.append(jnp.where(left, jnp.zeros_like(t), t))
        return jnp.concatenate(parts, axis=0)

    qi_stack = stack_heads(qi_ref, IDX_HEADS // 2)
    wit = wit_ref[...]
    krow = lax.broadcasted_iota(jnp.int32, (SPA_TK, SPA_TQ), 0)
    qpos = qb * SPA_TQ + lax.broadcasted_iota(jnp.int32, (SPA_TK, SPA_TQ), 1)

    def score_chunk(c, carry):
        k0 = pl.multiple_of(c * SPA_TK, SPA_TK)
        logits = _dot_nt(ki_ref[pl.ds(k0, SPA_TK), :], qi_stack)
        sc = jnp.zeros((SPA_TK, SPA_TQ), _F32)
        for hd in range(IDX_HEADS):
            sc = sc + wit[hd:hd + 1, :] * jnp.maximum(logits[:, SPA_TQ * hd:SPA_TQ * (hd + 1)], 0.0)
        sc = jnp.where((k0 + krow) <= qpos, sc, -jnp.inf)
        bits = pltpu.bitcast(sc, jnp.int32)
        keys = bits ^ ((bits >> 31) & jnp.int32(0x7FFFFFFF))
        key_scr[c] = keys
        words = (keys ^ jnp.int32(_INT_MIN)).reshape(_WORD_BITS, 8, SPA_TQ)
        for i, plane in enumerate(_bit_transpose32([words[v] for v in range(_WORD_BITS)])):
            plane_scr[c, i] = plane
        return carry

    def over_chunks(body, init):
        def two(i, carry):
            return body(2 * i + 1, body(2 * i, carry))
        carry = lax.fori_loop(0, n_chunks // 2, two, init)
        return lax.cond(n_chunks % 2 == 1, lambda cr: body(n_chunks - 1, cr), lambda cr: cr, carry)

    over_chunks(score_chunk, 0)

    def count(pred):
        def body(c, acc):
            hit = pred(key_scr[c], c * SPA_TK)
            return acc + jnp.sum(hit.astype(jnp.int32).reshape(SPA_TK // 8, 8, SPA_TQ), axis=0)
        acc = over_chunks(body, jnp.zeros((8, SPA_TQ), jnp.int32))
        return jnp.sum(acc, axis=0, keepdims=True)

    n_slots = key_scr.shape[0]

    def clear_planes(c, carry):
        plane_scr[c] = jnp.zeros(plane_scr.shape[1:], jnp.int32)
        return carry

    lax.fori_loop(n_chunks, n_slots, clear_planes, 0)
    eq = [jnp.full((8, SPA_TQ), jnp.where(c < n_chunks, -1, 0), jnp.int32) for c in range(n_slots)]
    gt = [jnp.zeros((8, SPA_TQ), jnp.int32) for _ in range(n_slots)]
    popcount = lambda words: jnp.sum(sum(lax.population_count(w) for w in words), axis=0, keepdims=True)
    t_u = jnp.zeros((1, SPA_TQ), jnp.int32)
    for i in reversed(range(_WORD_BITS)):
        planes = [plane_scr[c, i] for c in range(n_slots)]
        ones = [e & p for e, p in zip(eq, planes)]
        take = popcount([g | o for g, o in zip(gt, ones)]) >= n_keep
        t_u = jnp.where(take, t_u | jnp.int32(_INT_MIN if i == _WORD_BITS - 1 else 1 << i), t_u)
        eq = [jnp.where(take, o, e & ~p) for o, e, p in zip(ones, eq, planes)]
        gt = [jnp.where(take, g, g | o) for g, o in zip(gt, ones)]
    t = t_u ^ jnp.int32(_INT_MIN)
    n_gt = popcount(gt)
    n_ge = n_gt + popcount(eq)
    need = n_keep - n_gt
    n_bits = (key_scr.shape[0] * SPA_TK).bit_length()

    def search_last():
        def refine_pos(j, cand):
            below = count(lambda k, k0: (k == t) & ((k0 + krow) < cand))
            return jnp.where(below < need, cand, j)
        return lax.fori_loop(0, n_bits, lambda i, j: refine_pos(j, j | (jnp.int32(1) << (n_bits - 1 - i))),
                             jnp.zeros((1, SPA_TQ), jnp.int32))

    keep_all_ties = lambda: jnp.full((1, SPA_TQ), 2 ** n_bits, jnp.int32)
    last = lax.cond(jnp.max(n_ge) > n_keep, search_last, keep_all_ties)

    qs_scr[...] = stack_heads(q_ref, SPA_HEADS // 2) * (SPA_HD ** -0.5)
    m_scr[...] = jnp.full_like(m_scr, _NEG)
    acc_scr[...] = jnp.zeros_like(acc_scr)
    vrow = lax.broadcasted_iota(jnp.int32, (LANES, SPA_TK), 0)
    pair_w = 2 * SPA_TQ
    pairs = [slice(pair_w * pr, pair_w * (pr + 1)) for pr in range(SPA_HEADS // 2)]

    def score_pass(c, carry):
        k0 = pl.multiple_of(c * SPA_TK, SPA_TK)
        keys = key_scr[c]
        kpos = k0 + krow
        chosen = ((keys > t) | ((keys == t) & (kpos <= last))) & (kpos <= qpos)
        bias = jnp.where(chosen, 0.0, _NEG)
        bias2 = jnp.concatenate([bias, bias], axis=1)
        kk = kk_ref[pl.ds(k0, SPA_TK), :]
        for cols in pairs:
            st = _dot_nt(kk, qs_scr[cols, :]) + bias2
            s_scr[c, :, cols] = st
            m_scr[:, cols] = jnp.maximum(m_scr[:, cols], jnp.max(st, axis=0, keepdims=True))
        return carry

    def value_pass(c, carry):
        vt = vt_ref[c]
        v1t = jnp.where(vrow < SPA_HD, vt, jnp.ones_like(vt))
        for cols in pairs:
            p = jnp.exp(s_scr[c, :, cols] - m_scr[:, cols])
            acc_scr[:, cols] += _dot(v1t, p.astype(_MX))
        return carry

    lax.fori_loop(0, n_chunks, score_pass, 0)
    lax.fori_loop(0, n_chunks, value_pass, 0)
    for pr in range(SPA_HEADS // 2):
        a = acc_scr[:, pair_w * pr:pair_w * pr + SPA_TQ]
        b = acc_scr[:, pair_w * pr + SPA_TQ:pair_w * (pr + 1)]
        tile_t = jnp.concatenate([a[:SPA_HD] / a[SPA_HD:SPA_HD + 1], b[:SPA_HD] / b[SPA_HD:SPA_HD + 1]], axis=0)
        o_ref[:, LANES * pr:LANES * (pr + 1)] = tile_t.T.astype(o_ref.dtype)


def _sparse_attn(q, kk, vt, qi, ki, wit, batch, seq):
    n = q.shape[0]
    nq = seq // SPA_TQ
    n_keep = min(TOPK_MAX, seq // 4)
    return pl.pallas_call(
        functools.partial(_sparse_attn_kernel, n_keep=n_keep),
        grid=(batch, nq),
        in_specs=[
            pl.BlockSpec((SPA_TQ, SPA_HEADS * SPA_HD), lambda b, i: (b * nq + i, 0)),
            pl.BlockSpec((seq, LANES), lambda b, i: (b, 0)),
            pl.BlockSpec((seq // SPA_TK, LANES, SPA_TK), lambda b, i: (b, 0, 0)),
            pl.BlockSpec((SPA_TQ, IDX_HEADS * IDX_HD), lambda b, i: (b * nq + i, 0)),
            pl.BlockSpec((seq, LANES), lambda b, i: (b, 0)),
            pl.BlockSpec((IDX_HEADS, SPA_TQ), lambda b, i: (0, b * nq + i)),
        ],
        out_specs=pl.BlockSpec((SPA_TQ, SPA_HEADS * SPA_HD), lambda b, i: (b * nq + i, 0)),
        out_shape=jax.ShapeDtypeStruct((n, SPA_HEADS * SPA_HD), _MX),
        scratch_shapes=[pltpu.VMEM((seq // SPA_TK, SPA_TK, SPA_TQ), jnp.int32),
                        pltpu.VMEM((seq // SPA_TK, _WORD_BITS, 8, SPA_TQ), jnp.int32),
                        pltpu.VMEM((SPA_HEADS * SPA_TQ, LANES), _MX),
                        pltpu.VMEM((seq // SPA_TK, SPA_TK, SPA_HEADS * SPA_TQ), _F32),
                        pltpu.VMEM((1, SPA_HEADS * SPA_TQ), _F32),
                        pltpu.VMEM((LANES, SPA_HEADS * SPA_TQ), _F32)],
        compiler_params=_params(("parallel", "arbitrary")),
        name="sparse_attn",
    )(q, kk, vt, qi, ki, wit)


def _mem_kv_kernel(mem_ref, g_ref, w_ref, kg_ref, k_ref, v_ref):
    h = _row_rmsnorm(mem_ref[...], g_ref[...]).astype(_MX)
    kg = kg_ref[...]
    for hd in range(X_HEADS):
        cols = slice(X_HD * hd, X_HD * (hd + 1))
        y = _dot(h, w_ref[:, cols])
        k_ref[:, cols] = _row_rmsnorm(y, kg).astype(_MX)
    for hd in range(X_HEADS):
        cols = slice(X_HD * hd, X_HD * (hd + 1))
        v_ref[:, cols] = _dot(h, w_ref[:, D_MODEL + X_HD * hd:D_MODEL + X_HD * (hd + 1)]).astype(_MX)


def _mem_kv(mem, g, wkv, kg):
    n, m = mem.shape[0], 256
    row = pl.BlockSpec((m, D_MODEL), lambda i: (i, 0))
    return pl.pallas_call(
        _mem_kv_kernel,
        grid=(n // m,),
        in_specs=[row, _const_spec(g.shape), _const_spec(wkv.shape), _const_spec(kg.shape)],
        out_specs=[row, row],
        out_shape=[jax.ShapeDtypeStruct((n, D_MODEL), _MX)] * 2,
        compiler_params=_params(("parallel",)),
        name="mem_kv",
    )(mem, g, wkv, kg)


XATT_TM = 512


def _xattn_kernel(*refs, n_mix):
    x_ref = refs[0]
    mix_refs = refs[1:1 + n_mix]
    mixw_refs = refs[1 + n_mix:1 + 2 * n_mix]
    g_ref, wq_ref, qg_ref, k_ref, v_ref, wo_ref, out_ref, o_scr = refs[1 + 2 * n_mix:]
    x = x_ref[...]
    for a_ref, w_ref in zip(mix_refs, mixw_refs):
        x = x + _dot(a_ref[...], w_ref[...])
    h = _row_rmsnorm(x, g_ref[...]).astype(_MX)
    qg = qg_ref[...]
    scale = X_HD ** -0.5
    heads = [slice(X_HD * hd, X_HD * (hd + 1)) for hd in range(X_HEADS)]
    qs = [_dot(h, wq_ref[:, cols]) for cols in heads]
    qs = [_row_rmsnorm(q, qg).astype(_MX) for q in qs]
    ss = [_dot_nt(q, k_ref[:, cols]) * scale for q, cols in zip(qs, heads)]
    for s, cols in zip(ss, heads):
        p = jnp.exp(s - jnp.max(s, axis=-1, keepdims=True))
        p = p / jnp.sum(p, axis=-1, keepdims=True)
        o_scr[:, cols] = _dot(p.astype(_MX), v_ref[:, cols]).astype(_MX)
    out_ref[...] = x + _dot(o_scr[...], wo_ref[...])


def _xattn(x, mix, mix_w, g, wq, qg, k, v, wo, batch, seq, mem_len):
    n = x.shape[0]
    tm = XATT_TM
    nt = seq // tm
    row = lambda w: pl.BlockSpec((tm, w), lambda b, i: (b * nt + i, 0))
    const = lambda a: pl.BlockSpec(a.shape, lambda b, i: (0,) * a.ndim)
    memspec = pl.BlockSpec((mem_len, D_MODEL), lambda b, i: (b, 0))
    return pl.pallas_call(
        functools.partial(_xattn_kernel, n_mix=len(mix)),
        grid=(batch, nt),
        in_specs=([row(D_MODEL)] + [row(a.shape[1]) for a in mix] + [const(w) for w in mix_w]
                  + [const(g), const(wq), const(qg), memspec, memspec, const(wo)]),
        out_specs=row(D_MODEL),
        out_shape=jax.ShapeDtypeStruct(x.shape, _F32),
        scratch_shapes=[pltpu.VMEM((tm, D_MODEL), _MX)],
        compiler_params=_params(("parallel", "parallel")),
        name="xattn",
    )(x, *mix, *mix_w, g, wq, qg, k, v, wo)


def _seg_matrix(seg):
    idx = np.arange(MXU_WIDTH) // seg
    return jnp.asarray(idx[:, None] == idx[None, :], _MX)


def _rope_tables(positions, dim):
    inv = jnp.exp(-math.log(ROPE_THETA) * jnp.arange(0, dim, 2, dtype=_F32) / dim)
    ang = positions.astype(_F32).reshape(-1)[:, None] * inv
    return jnp.cos(ang), jnp.sin(ang)


def _tables(positions):
    cos64, sin64 = _rope_tables(positions, DIL_HD)
    cos32, sin32 = _rope_tables(positions, MLA_ROPE)
    n = cos64.shape[0]
    c64 = jnp.tile(cos64, (1, 4))
    n64 = jnp.tile(jnp.concatenate([-sin64, sin64], axis=1), (1, 2))
    ones, zeros = jnp.ones((n, MLA_NOPE), _F32), jnp.zeros((n, MLA_NOPE), _F32)
    pad1, pad0 = jnp.ones((n, 32), _F32), jnp.zeros((n, 32), _F32)
    cm = jnp.concatenate([ones, cos32, cos32, pad1], axis=1)
    nm = jnp.concatenate([zeros, -sin32, sin32, pad0], axis=1)
    return c64, n64, cm, nm


def _even_weights(w_in, q_lora_g, kv_lora_g, w_uq, w_ukv, q_g, k_g, dq_g, dk_g, w_out):
    o1, o2, o3 = MLA_Q_LORA, MLA_Q_LORA + MLA_KV_LORA, MLA_Q_LORA + MLA_KV_LORA + MLA_ROPE
    d = w_in.shape[0]
    z = lambda w: jnp.zeros((d, w), _F32)
    win = jnp.concatenate([w_in[:, :o2], z(64), w_in[:, o2:o3], z(32), w_in[:, o3:]], axis=1).astype(_MX)
    ones = lambda w: jnp.ones((w,), _F32)
    zeros = lambda w: jnp.zeros((w,), _F32)
    grp_gain = jnp.concatenate([jnp.tile(dq_g, DIL_HEADS), jnp.tile(dk_g, DIL_HEADS), ones(256)])
    pin = jnp.stack([
        jnp.concatenate([ones(256) / 256, ones(128) / 128, ones(128) / MLA_ROPE, ones(DIL_COLS) / DIL_HD]),
        jnp.concatenate([q_lora_g, kv_lora_g, zeros(64), k_g[MLA_NOPE:], zeros(32), jnp.tile(grp_gain, 3)]),
    ])
    wq = jnp.pad(w_uq.reshape(MLA_Q_LORA, MLA_HEADS, MLA_NOPE + MLA_ROPE), ((0, 0), (0, 0), (0, 32)))
    wq = wq.reshape(MLA_Q_LORA, MLA_HEADS * LANES).astype(_MX)
    pq = jnp.stack([
        jnp.tile(jnp.concatenate([ones(64) / MLA_NOPE, ones(64) / MLA_ROPE]), MLA_HEADS),
        jnp.tile(jnp.concatenate([q_g, zeros(32)]), MLA_HEADS),
    ])
    ukv = w_ukv.reshape(MLA_KV_LORA, MLA_HEADS, MLA_NOPE + MLA_V)
    wk = jnp.pad(ukv[:, :, :MLA_NOPE], ((0, 0), (0, 0), (0, 64))).reshape(MLA_KV_LORA, MLA_HEADS * LANES).astype(_MX)
    wv = ukv[:, :, MLA_NOPE:].reshape(MLA_KV_LORA, MLA_HEADS * MLA_V).astype(_MX)
    pk = jnp.stack([
        jnp.tile(ones(LANES) / MLA_NOPE, MLA_HEADS),
        jnp.tile(jnp.concatenate([k_g[:MLA_NOPE], zeros(64)]), MLA_HEADS),
    ])
    return win, wq, wk, wv, pin, pq, pk, w_out.astype(_MX)


def _odd_weights(w_in, q_g, k_g, w_out):
    o1 = SPA_HEADS * SPA_HD
    o2, o3 = o1 + SPA_HD, o1 + 2 * SPA_HD
    o4 = o3 + IDX_HEADS * IDX_HD
    o5 = o4 + IDX_HD
    d = w_in.shape[0]
    k, v, ki = w_in[:, o1:o2], w_in[:, o2:o3], w_in[:, o4:o5]
    win = jnp.concatenate([w_in[:, :o1], k, k, v, v, w_in[:, o3:o4], ki, ki, w_in[:, o5:],
                           jnp.zeros((d, LANES - IDX_HEADS), _F32)], axis=1).astype(_MX)
    ones = jnp.ones((ODD_IN_PAD,), _F32)
    gain = jnp.concatenate([jnp.tile(q_g, SPA_HEADS), k_g, k_g, jnp.ones((ODD_IN_PAD - o1 - 2 * SPA_HD,), _F32)])
    pin = jnp.stack([ones / SPA_HD, gain])
    return win, pin, w_out.astype(_MX)


def kernel(x, mem, positions, ffn1_norm, ffn1_w13, ffn1_w2, mix_norm, xattn_norm, mem_norm,
           xattn_wq, xattn_wkv, xattn_q_gain, xattn_k_gain, xattn_wo, ffn2_norm, ffn2_w13, ffn2_w2,
           even_w_in, mla_q_lora_norm, mla_kv_lora_norm, mla_w_uq, mla_w_ukv, mla_q_gain, mla_k_gain,
           dil_q_gain, dil_k_gain, even_w_out, odd_w_in, sparse_q_gain, sparse_k_gain, odd_w_out):
    batch, seq, d = x.shape
    mem_len = mem.shape[1]
    n = batch * seq
    c64, n64, cm, nm = _tables(positions)
    s64, s128, s256 = _seg_matrix(64), _seg_matrix(128), _seg_matrix(256)
    bf = lambda a: a.astype(_MX)
    row3 = lambda a: a.reshape(a.shape[0], 1, a.shape[1])
    f1w13, f1w2, f2w13, f2w2 = bf(ffn1_w13), bf(ffn1_w2), bf(ffn2_w13), bf(ffn2_w2)
    f1g, f2g = row3(ffn1_norm), row3(ffn2_norm)
    xwq, xwkv, xwo = bf(xattn_wq), bf(xattn_wkv), bf(xattn_wo)

    xs = x.reshape(n, d)
    mems = mem.reshape(batch * mem_len, d)
    for i in range(DEPTH):
        j = i // 2
        xs = _ffn(xs, f1g, f1w13, f1w2, i)
        g_mix = mix_norm[i][None, :]
        if i % 2 == 0:
            win, wq, wk, wv, pin, pq, pk, wout = _even_weights(
                even_w_in[j], mla_q_lora_norm[j], mla_kv_lora_norm[j], mla_w_uq[j], mla_w_ukv[j],
                mla_q_gain[j], mla_k_gain[j], dil_q_gain[j], dil_k_gain[j], even_w_out[j])
            qm, km, vt, dil = _even_proj(xs, g_mix, win, wq, wk, wv, pin, pq, pk,
                                         (s64, s128, s256), (c64, n64, cm, nm))
            n_mla = MLA_HEADS * MLA_V
            mix = [_mla_attn(qm, km, vt, batch, seq), _dil_attn(dil, batch, seq)]
            mix_w = [wout[:n_mla], wout[n_mla:]]
        else:
            win, pin, wout = _odd_weights(odd_w_in[j], sparse_q_gain[j], sparse_k_gain[j], odd_w_out[j])
            q, kk, vt, qi, ki, wit = _odd_proj(xs, g_mix, win, pin, s64, (c64, n64))
            mix = [_sparse_attn(q, kk, vt, qi, ki, wit, batch, seq)]
            mix_w = [wout]
        mk, mv = _mem_kv(mems, mem_norm[i][None, :], xwkv[i], xattn_k_gain[i][None, :])
        xs = _xattn(xs, mix, mix_w, xattn_norm[i][None, :], xwq[i], xattn_q_gain[i][None, :], mk, mv, xwo[i],
                    batch, seq, mem_len)
        xs = _ffn(xs, f2g, f2w13, f2w2, i)
    return xs.reshape(batch, seq, d)
```

```python
import functools
import math

import jax
import jax.numpy as jnp
import numpy as np
from jax import lax
from jax.experimental import pallas as pl
from jax.experimental.pallas import tpu as pltpu

D_MODEL = 1024
DEPTH = 4
EPS = 1e-6
D_FF = 2816
ROPE_THETA = 10000.0
MLA_HEADS, MLA_NOPE, MLA_ROPE, MLA_V = 8, 64, 32, 64
MLA_Q_LORA, MLA_KV_LORA = 256, 128
DIL_PATTERNS = ((128, 1), (512, 4), (2048, 16))
DIL_HEADS, DIL_HD = 4, 64
SPA_HEADS, SPA_HD = 16, 64
IDX_HEADS, IDX_HD = 8, 64
TOPK_MAX = 256
X_HEADS = 4
X_HD = D_MODEL // X_HEADS

LANES = 128
MXU_WIDTH = 256
VMEM_LIMIT_BYTES = 56 * 1024 * 1024

_MX = jnp.bfloat16
_F32 = jnp.float32
_NEG = -1e30


def _dot(a, b):
    return jnp.dot(a, b, preferred_element_type=_F32)


def _dot_nt(a, b):
    return lax.dot_general(a, b, (((1,), (1,)), ((), ())), preferred_element_type=_F32)


def _params(sem, flags=None):
    return pltpu.CompilerParams(dimension_semantics=sem, vmem_limit_bytes=VMEM_LIMIT_BYTES, flags=flags)


def _row_rmsnorm(x, g):
    return x * lax.rsqrt(jnp.mean(x * x, axis=-1, keepdims=True) + EPS) * g


def _seg_sumsq(y, seg_ref):
    return _dot((y * y).astype(_MX), seg_ref[...])


def _swap_lanes(z, d):
    lane = lax.broadcasted_iota(jnp.int32, z.shape, 1)
    fwd = pltpu.roll(z, LANES - d, 1)
    bwd = pltpu.roll(z, d, 1)
    return jnp.where((lane & d) == 0, fwd, bwd)


def _rope_tile(z, d, cos, sin_signed):
    return z * cos + _swap_lanes(z, d) * sin_signed


def _run_pipelined(jobs):
    y_next = jobs[0][0]()
    for i, (_, epilogue) in enumerate(jobs):
        y = y_next
        if i + 1 < len(jobs):
            y_next = jobs[i + 1][0]()
        epilogue(y)


FFN_TM = 512
FFN_CHUNKS = (768, 768, 768, 512)
assert sum(FFN_CHUNKS) == D_FF and all(c % MXU_WIDTH == 0 for c in FFN_CHUNKS)


def _ffn_kernel(x_ref, g_ref, w13_ref, w2_ref, o_ref, act_scr):
    x = x_ref[...]
    h = _row_rmsnorm(x, g_ref[...]).astype(_MX)
    c0 = 0
    for width in FFN_CHUNKS:
        gate = _dot(h, w13_ref[:, c0:c0 + width])
        up = _dot(h, w13_ref[:, D_FF + c0:D_FF + c0 + width])
        act_scr[:, c0:c0 + width] = (jax.nn.silu(gate) * up).astype(_MX)
        c0 += width
    o_ref[...] = x + 0.5 * _dot(act_scr[...], w2_ref[...])


def _ffn(x, g, w13, w2, layer):
    n = x.shape[0]
    resident = pl.Buffered(1)
    return pl.pallas_call(
        _ffn_kernel,
        grid=(n // FFN_TM,),
        in_specs=[
            pl.BlockSpec((FFN_TM, D_MODEL), lambda i: (i, 0)),
            pl.BlockSpec((None, 1, D_MODEL), lambda i: (layer, 0, 0)),
            pl.BlockSpec((None, D_MODEL, 2 * D_FF), lambda i: (layer, 0, 0), pipeline_mode=resident),
            pl.BlockSpec((None, D_FF, D_MODEL), lambda i: (layer, 0, 0), pipeline_mode=resident),
        ],
        out_specs=pl.BlockSpec((FFN_TM, D_MODEL), lambda i: (i, 0)),
        out_shape=jax.ShapeDtypeStruct(x.shape, _F32),
        scratch_shapes=[pltpu.VMEM((FFN_TM, D_FF), _MX)],
        compiler_params=_params(("parallel",)),
        name="ffn",
    )(x, g, w13, w2)


PROJ_TM = 256
EVEN_IN_PAD = 2816
DIL_COLS = 2304


def _even_proj_kernel(x_ref, g_ref, win_ref, wq_ref, wk_ref, wv_ref, pin_ref, pq_ref, pk_ref,
                      s64_ref, s128_ref, s256_ref, c64_ref, n64_ref, cm_ref, nm_ref,
                      qm_ref, km_ref, vt_ref, dil_ref):
    h = _row_rmsnorm(x_ref[...], g_ref[...]).astype(_MX)
    c64, n64, cm, nm = c64_ref[...], n64_ref[...], cm_ref[...], nm_ref[...]

    def normed(y, seg_ref, p_ref, c0):
        ss = _seg_sumsq(y, seg_ref)
        return y * lax.rsqrt(ss * p_ref[0:1, c0:c0 + MXU_WIDTH] + EPS) * p_ref[1:2, c0:c0 + MXU_WIDTH]

    tiles = lambda c0: [(t, slice(c0 + LANES * t, c0 + LANES * (t + 1))) for t in range(2)]
    latent = {}
    jobs = []

    def epi_cq(y):
        latent["cq"] = normed(y, s256_ref, pin_ref, 0).astype(_MX)
    jobs.append((lambda: _dot(h, win_ref[:, 0:256]), epi_cq))

    def epi_ckv(y):
        z = normed(y, s128_ref, pin_ref, 256)
        latent["ckv"] = z[:, :LANES].astype(_MX)
        latent["k_rope"] = _rope_tile(z[:, LANES:], MLA_ROPE // 2, cm, nm)
    jobs.append((lambda: _dot(h, win_ref[:, 256:512]), epi_ckv))

    for j in range(DIL_COLS // MXU_WIDTH):
        def epi_dil(y, j=j):
            if j % 3 < 2:
                z = normed(y, s64_ref, pin_ref, 512 + MXU_WIDTH * j)
                for t, cols in tiles(MXU_WIDTH * j):
                    dil_ref[:, cols] = _rope_tile(z[:, LANES * t:LANES * (t + 1)], DIL_HD // 2, c64, n64)
            else:
                dil_ref[:, MXU_WIDTH * j:MXU_WIDTH * (j + 1)] = y
        jobs.append((lambda j=j: _dot(h, win_ref[:, 512 + MXU_WIDTH * j:512 + MXU_WIDTH * (j + 1)]), epi_dil))

    for j in range(MLA_HEADS * LANES // MXU_WIDTH):
        c0 = MXU_WIDTH * j
        def epi_q(y, c0=c0):
            z = normed(y, s64_ref, pq_ref, c0)
            for t, cols in tiles(c0):
                qm_ref[:, cols] = _rope_tile(z[:, LANES * t:LANES * (t + 1)], MLA_ROPE // 2, cm, nm).astype(_MX)
        jobs.append((lambda c0=c0: _dot(latent["cq"], wq_ref[:, c0:c0 + MXU_WIDTH]), epi_q))

        def epi_k(y, c0=c0):
            z = normed(y, s64_ref, pk_ref, c0)
            for t, cols in tiles(c0):
                km_ref[:, cols] = (z[:, LANES * t:LANES * (t + 1)] + latent["k_rope"]).astype(_MX)
        jobs.append((lambda c0=c0: _dot(latent["ckv"], wk_ref[:, c0:c0 + MXU_WIDTH]), epi_k))

    for j in range(MLA_HEADS * MLA_V // MXU_WIDTH):
        c0 = MXU_WIDTH * j
        def epi_v(y, c0=c0):
            for t, cols in tiles(c0):
                vt_ref[cols, :] = y[:, LANES * t:LANES * (t + 1)].T.astype(_MX)
        jobs.append((lambda c0=c0: _dot(latent["ckv"], wv_ref[:, c0:c0 + MXU_WIDTH]), epi_v))
    _run_pipelined(jobs)


def _const_spec(shape):
    nd = len(shape)
    return pl.BlockSpec(shape, lambda i: (0,) * nd)


def _even_proj(x, g, win, wq, wk, wv, pin, pq, pk, segs, tabs):
    n = x.shape[0]
    tm = PROJ_TM
    row = lambda w: pl.BlockSpec((tm, w), lambda i: (i, 0))
    s64, s128, s256 = segs
    c64, n64, cm, nm = tabs
    return pl.pallas_call(
        _even_proj_kernel,
        grid=(n // tm,),
        in_specs=[row(D_MODEL), _const_spec(g.shape), _const_spec(win.shape), _const_spec(wq.shape),
                  _const_spec(wk.shape), _const_spec(wv.shape), _const_spec(pin.shape),
                  _const_spec(pq.shape), _const_spec(pk.shape), _const_spec(s64.shape),
                  _const_spec(s128.shape), _const_spec(s256.shape),
                  row(LANES), row(LANES), row(LANES), row(LANES)],
        out_specs=[row(MLA_HEADS * LANES), row(MLA_HEADS * LANES),
                   pl.BlockSpec((None, MLA_HEADS * MLA_V, tm), lambda i: (i, 0, 0)), row(DIL_COLS)],
        out_shape=[jax.ShapeDtypeStruct((n, MLA_HEADS * LANES), _MX),
                   jax.ShapeDtypeStruct((n, MLA_HEADS * LANES), _MX),
                   jax.ShapeDtypeStruct((n // tm, MLA_HEADS * MLA_V, tm), _MX),
                   jax.ShapeDtypeStruct((n, DIL_COLS), _F32)],
        compiler_params=_params(("parallel",)),
        name="even_proj",
    )(x, g, win, wq, wk, wv, pin, pq, pk, s64, s128, s256, c64, n64, cm, nm)


MLA_TQ = 256
MLA_TK = 256


def _mla_attn_kernel(q_ref, k_ref, vt_ref, o_ref, s_scr, m_scr, acc_scr):
    qi = pl.program_id(1)
    scale = (MLA_NOPE + MLA_ROPE) ** -0.5
    m_scr[...] = jnp.full_like(m_scr, _NEG)
    acc_scr[...] = jnp.zeros_like(acc_scr)
    kpos = lax.broadcasted_iota(jnp.int32, (MLA_TK, MLA_TQ), 0)
    qcol = lax.broadcasted_iota(jnp.int32, (MLA_TK, MLA_TQ), 1)

    def score_chunks(c, n, diagonal):
        k0 = pl.multiple_of(c * MLA_TK, MLA_TK)
        for hd in range(MLA_HEADS):
            cols = slice(LANES * hd, LANES * (hd + 1))
            st = _dot_nt(k_ref[pl.ds(k0, n * MLA_TK), cols], q_ref[:, cols])
            if diagonal:
                st = jnp.where(kpos <= qcol, st, _NEG)
            for j in range(n):
                s_scr[c + j, hd] = st[MLA_TK * j:MLA_TK * (j + 1)]
            m_scr[hd] = jnp.maximum(m_scr[hd], jnp.max(st, axis=0, keepdims=True))

    def score_pass(i, carry):
        score_chunks(2 * i, 2, False)
        return carry

    ones = jnp.ones((MLA_V, MLA_TK), _MX)

    def value_pass(c, carry):
        for hd in range(MLA_HEADS):
            p = jnp.exp2((s_scr[c, hd] - m_scr[hd]) * (scale * math.log2(math.e)))
            v1t = jnp.concatenate([vt_ref[c, MLA_V * hd:MLA_V * (hd + 1), :], ones], axis=0)
            acc_scr[hd] += _dot(v1t, p.astype(_MX))
        return carry

    lax.fori_loop(0, qi // 2, score_pass, 0)

    @pl.when(qi % 2 == 1)
    def _():
        score_chunks(qi - 1, 1, False)

    score_chunks(qi, 1, True)
    lax.fori_loop(0, qi + 1, value_pass, 0)
    ot = jnp.concatenate([acc_scr[hd, :MLA_V] / acc_scr[hd, MLA_V:MLA_V + 1] for hd in range(MLA_HEADS)],
                         axis=0)
    o_ref[...] = ot.T.astype(o_ref.dtype)


def _mla_attn(qm, km, vt, batch, seq):
    n = qm.shape[0]
    nq = seq // MLA_TQ
    wv = MLA_HEADS * MLA_V
    return pl.pallas_call(
        _mla_attn_kernel,
        grid=(batch, nq),
        in_specs=[
            pl.BlockSpec((MLA_TQ, MLA_HEADS * LANES), lambda b, i: (b * nq + i, 0)),
            pl.BlockSpec((seq, MLA_HEADS * LANES), lambda b, i: (b, 0)),
            pl.BlockSpec((seq // MLA_TK, wv, MLA_TK), lambda b, i: (b, 0, 0)),
        ],
        out_specs=pl.BlockSpec((MLA_TQ, wv), lambda b, i: (b * nq + i, 0)),
        out_shape=jax.ShapeDtypeStruct((n, wv), _MX),
        scratch_shapes=[pltpu.VMEM((seq // MLA_TK, MLA_HEADS, MLA_TK, MLA_TQ), _F32),
                        pltpu.VMEM((MLA_HEADS, 1, MLA_TQ), _F32),
                        pltpu.VMEM((MLA_HEADS, 2 * MLA_V, MLA_TQ), _F32)],
        compiler_params=_params(("parallel", "arbitrary")),
        name="mla_attn",
    )(qm, km, vt)


DIL_BLK = 128
DIL_UNROLL = 8


def _dil_attn_kernel(*refs, seq):
    n_grp = len(DIL_PATTERNS)
    qkv = [refs[3 * g:3 * g + 3] for g in range(n_grp)]
    out_ref = refs[3 * n_grp]
    o_slabs = refs[3 * n_grp + 1:3 * n_grp + 1 + n_grp]
    l_slabs = refs[3 * n_grp + 1 + n_grp:]
    scale = DIL_HD ** -0.5
    kidx = lax.broadcasted_iota(jnp.int32, (DIL_BLK, DIL_BLK), 0)
    qidx = lax.broadcasted_iota(jnp.int32, (DIL_BLK, DIL_BLK), 1)
    lane = lax.broadcasted_iota(jnp.int32, (DIL_BLK, LANES), 1)
    dim = lax.broadcasted_iota(jnp.int32, (LANES, DIL_BLK), 0)
    cur_ok = kidx <= qidx

    for g, (window, d) in enumerate(DIL_PATTERNS):
        q_ref, k_ref, v_ref = qkv[g]
        nb = seq // d // DIL_BLK

        def units(it, carry, d=d, nb=nb, q_ref=q_ref, k_ref=k_ref, v_ref=v_ref, g=g):
            staged = []
            for u in range(DIL_UNROLL):
                idx = it * DIL_UNROLL + u
                r = idx // nb
                n = idx % nb
                rows = pl.ds(r + d * DIL_BLK * n, DIL_BLK, stride=d)
                prev = pl.ds(r + d * DIL_BLK * jnp.maximum(n - 1, 0), DIL_BLK, stride=d)
                prev_ok = kidx >= qidx + jnp.where(n > 0, 0, DIL_BLK)
                q2 = q_ref[rows, :] * scale
                kc = k_ref[rows, :].astype(_MX)
                kp = k_ref[prev, :].astype(_MX)
                halves = []
                for half in range(2):
                    mine = (lane < DIL_HD) if half == 0 else (lane >= DIL_HD)
                    qh = jnp.where(mine, q2, 0.0).astype(_MX)
                    halves.append((_dot_nt(kc, qh), _dot_nt(kp, qh)))
                staged.append((rows, prev, prev_ok, halves))
            for rows, prev, prev_ok, halves in staged:
                vct = v_ref[rows, :].T.astype(_MX)
                vpt = v_ref[prev, :].T.astype(_MX)
                ots, lses = [], []
                for sc, sp in halves:
                    sc = jnp.where(cur_ok, sc, _NEG)
                    sp = jnp.where(prev_ok, sp, _NEG)
                    m = jnp.maximum(jnp.max(sc, axis=0, keepdims=True), jnp.max(sp, axis=0, keepdims=True))
                    pc = jnp.exp(sc - m)
                    pp = jnp.exp(sp - m)
                    l = jnp.sum(pc, axis=0, keepdims=True) + jnp.sum(pp, axis=0, keepdims=True)
                    ots.append((_dot(vct, pc.astype(_MX)) + _dot(vpt, pp.astype(_MX))) / l)
                    lses.append(m + jnp.log(l))
                ot = jnp.where(dim < DIL_HD, ots[0], ots[1])
                lt = jnp.where(dim < DIL_HD, lses[0], lses[1])
                o_slabs[g][rows, :] = ot.T
                l_slabs[g][rows, :] = lt.T
            return carry

        lax.fori_loop(0, d * nb // DIL_UNROLL, units, 0)

    ls = [l_slabs[g][...] for g in range(n_grp)]
    m = functools.reduce(jnp.maximum, ls)
    es = [jnp.exp(l - m) for l in ls]
    num = sum(e * o_slabs[g][...] for g, e in enumerate(es))
    out_ref[...] = (num / sum(es)).astype(out_ref.dtype)


def _dil_attn(dil, batch, seq):
    n = dil.shape[0]
    pairs = DIL_HEADS // 2
    spec = lambda g, part: pl.BlockSpec((seq, LANES), lambda b, p: (b, (3 * g + part) * pairs + p))
    in_specs = [spec(g, part) for g in range(len(DIL_PATTERNS)) for part in range(3)]
    return pl.pallas_call(
        functools.partial(_dil_attn_kernel, seq=seq),
        grid=(batch, pairs),
        in_specs=in_specs,
        out_specs=pl.BlockSpec((seq, LANES), lambda b, p: (b, p)),
        out_shape=jax.ShapeDtypeStruct((n, DIL_HEADS * DIL_HD), _MX),
        scratch_shapes=[pltpu.VMEM((seq, LANES), _F32)] * (2 * len(DIL_PATTERNS)),
        compiler_params=_params(("parallel", "parallel")),
        name="dil_attn",
    )(*([dil] * len(in_specs)))


ODD_IN_PAD = 2048


def _odd_proj_kernel(x_ref, g_ref, win_ref, pin_ref, s64_ref, c64_ref, n64_ref,
                     q_ref, kk_ref, vt_ref, qi_ref, ki_ref, wit_ref):
    h = _row_rmsnorm(x_ref[...], g_ref[...]).astype(_MX)
    c64, n64 = c64_ref[...], n64_ref[...]

    def normed(y, c0):
        ss = _seg_sumsq(y, s64_ref)
        return y * lax.rsqrt(ss * pin_ref[0:1, c0:c0 + MXU_WIDTH] + EPS) * pin_ref[1:2, c0:c0 + MXU_WIDTH]

    def rope64(t):
        return _rope_tile(t, SPA_HD // 2, c64, n64)

    nq = SPA_HEADS * SPA_HD
    c_qi = nq + MXU_WIDTH
    c_ki = c_qi + IDX_HEADS * IDX_HD
    product = lambda c0: (lambda: _dot(h, win_ref[:, c0:c0 + MXU_WIDTH]))
    jobs = []
    for j in range(nq // MXU_WIDTH):
        def epi_q(y, c0=MXU_WIDTH * j):
            z = normed(y, c0)
            for t in range(2):
                q_ref[:, c0 + LANES * t:c0 + LANES * (t + 1)] = rope64(z[:, LANES * t:LANES * (t + 1)]).astype(_MX)
        jobs.append((product(MXU_WIDTH * j), epi_q))

    def epi_kv(y):
        z = normed(y, nq)
        kk_ref[...] = rope64(z[:, :LANES]).astype(_MX)
        vt_ref[...] = y[:, LANES:].T.astype(_MX)
    jobs.append((product(nq), epi_kv))

    for j in range(IDX_HEADS * IDX_HD // MXU_WIDTH):
        def epi_qi(y, o0=MXU_WIDTH * j):
            for t in range(2):
                qi_ref[:, o0 + LANES * t:o0 + LANES * (t + 1)] = rope64(y[:, LANES * t:LANES * (t + 1)]).astype(_MX)
        jobs.append((product(c_qi + MXU_WIDTH * j), epi_qi))

    def epi_ki(y):
        ki_ref[...] = rope64(y[:, :LANES]).astype(_MX)
        wi = (y[:, LANES:] * (IDX_HEADS ** -0.5)) * (IDX_HD ** -0.5)
        wit_ref[...] = wi.T[0:IDX_HEADS, :]
    jobs.append((product(c_ki), epi_ki))
    _run_pipelined(jobs)


def _odd_proj(x, g, win, pin, s64, tabs):
    n = x.shape[0]
    tm = PROJ_TM
    row = lambda w: pl.BlockSpec((tm, w), lambda i: (i, 0))
    c64, n64 = tabs
    return pl.pallas_call(
        _odd_proj_kernel,
        grid=(n // tm,),
        in_specs=[row(D_MODEL), _const_spec(g.shape), _const_spec(win.shape), _const_spec(pin.shape),
                  _const_spec(s64.shape), row(LANES), row(LANES)],
        out_specs=[row(SPA_HEADS * SPA_HD), row(LANES),
                   pl.BlockSpec((None, LANES, tm), lambda i: (i, 0, 0)),
                   row(IDX_HEADS * IDX_HD), row(LANES),
                   pl.BlockSpec((IDX_HEADS, tm), lambda i: (0, i))],
        out_shape=[jax.ShapeDtypeStruct((n, SPA_HEADS * SPA_HD), _MX),
                   jax.ShapeDtypeStruct((n, LANES), _MX),
                   jax.ShapeDtypeStruct((n // tm, LANES, tm), _MX),
                   jax.ShapeDtypeStruct((n, IDX_HEADS * IDX_HD), _MX),
                   jax.ShapeDtypeStruct((n, LANES), _MX),
                   jax.ShapeDtypeStruct((IDX_HEADS, n), _F32)],
        compiler_params=_params(("parallel",)),
        name="odd_proj",
    )(x, g, win, pin, s64, c64, n64)


SPA_TQ = 128
SPA_TK = 256
_INT_MIN = -2 ** 31
assert SPA_TK == PROJ_TM and MLA_TK == PROJ_TM and MLA_TQ == MLA_TK


_WORD_BITS = 32
assert SPA_TK == 8 * _WORD_BITS


def _bit_transpose32(words):
    words = list(words)
    for shift, mask in ((16, 0x0000FFFF), (8, 0x00FF00FF), (4, 0x0F0F0F0F), (2, 0x33333333), (1, 0x55555555)):
        for k in range(_WORD_BITS):
            if k & shift == 0:
                t = (lax.shift_right_logical(words[k], jnp.int32(shift)) ^ words[k + shift]) & jnp.int32(mask)
                words[k] = words[k] ^ (t << shift)
                words[k + shift] = words[k + shift] ^ t
    return words


def _sparse_attn_kernel(q_ref, kk_ref, vt_ref, qi_ref, ki_ref, wit_ref, o_ref,
                        key_scr, plane_scr, qs_scr, s_scr, m_scr, acc_scr, *, n_keep):
    qb = pl.program_id(1)
    n_chunks = qb // (SPA_TK // SPA_TQ) + 1
    lane = lax.broadcasted_iota(jnp.int32, (SPA_TQ, LANES), 1)
    left = lane < SPA_HD

    def stack_heads(ref, n_pairs):
        parts = []
        for pr in range(n_pairs):
            t = ref[:, LANES * pr:LANES * (pr + 1)]
            parts.append(jnp.where(left, t, jnp.zeros_like(t)))
            parts.append(jnp.where(left, jnp.zeros_like(t), t))
        return jnp.concatenate(parts, axis=0)

    qi_stack = stack_heads(qi_ref, IDX_HEADS // 2)
    wit = wit_ref[...]
    krow = lax.broadcasted_iota(jnp.int32, (SPA_TK, SPA_TQ), 0)
    qpos = qb * SPA_TQ + lax.broadcasted_iota(jnp.int32, (SPA_TK, SPA_TQ), 1)

    def score_chunk(c, carry):
        k0 = pl.multiple_of(c * SPA_TK, SPA_TK)
        logits = _dot_nt(ki_ref[pl.ds(k0, SPA_TK), :], qi_stack)
        sc = jnp.zeros((SPA_TK, SPA_TQ), _F32)
        for hd in range(IDX_HEADS):
            sc = sc + wit[hd:hd + 1, :] * jnp.maximum(logits[:, SPA_TQ * hd:SPA_TQ * (hd + 1)], 0.0)
        sc = jnp.where((k0 + krow) <= qpos, sc, -jnp.inf)
        bits = pltpu.bitcast(sc, jnp.int32)
        keys = bits ^ ((bits >> 31) & jnp.int32(0x7FFFFFFF))
        key_scr[c] = keys
        words = (keys ^ jnp.int32(_INT_MIN)).reshape(_WORD_BITS, 8, SPA_TQ)
        for i, plane in enumerate(_bit_transpose32([words[v] for v in range(_WORD_BITS)])):
            plane_scr[c, i] = plane
        return carry

    def over_chunks(body, init):
        def two(i, carry):
            return body(2 * i + 1, body(2 * i, carry))
        carry = lax.fori_loop(0, n_chunks // 2, two, init)
        return lax.cond(n_chunks % 2 == 1, lambda cr: body(n_chunks - 1, cr), lambda cr: cr, carry)

    over_chunks(score_chunk, 0)

    def count(pred):
        def body(c, acc):
            hit = pred(key_scr[c], c * SPA_TK)
            return acc + jnp.sum(hit.astype(jnp.int32).reshape(SPA_TK // 8, 8, SPA_TQ), axis=0)
        acc = over_chunks(body, jnp.zeros((8, SPA_TQ), jnp.int32))
        return jnp.sum(acc, axis=0, keepdims=True)

    n_slots = key_scr.shape[0]

    def clear_planes(c, carry):
        plane_scr[c] = jnp.zeros(plane_scr.shape[1:], jnp.int32)
        return carry

    lax.fori_loop(n_chunks, n_slots, clear_planes, 0)
    eq = [jnp.full((8, SPA_TQ), jnp.where(c < n_chunks, -1, 0), jnp.int32) for c in range(n_slots)]
    gt = [jnp.zeros((8, SPA_TQ), jnp.int32) for _ in range(n_slots)]
    popcount = lambda words: jnp.sum(sum(lax.population_count(w) for w in words), axis=0, keepdims=True)
    t_u = jnp.zeros((1, SPA_TQ), jnp.int32)
    for i in reversed(range(_WORD_BITS)):
        planes = [plane_scr[c, i] for c in range(n_slots)]
        ones = [e & p for e, p in zip(eq, planes)]
        take = popcount([g | o for g, o in zip(gt, ones)]) >= n_keep
        t_u = jnp.where(take, t_u | jnp.int32(_INT_MIN if i == _WORD_BITS - 1 else 1 << i), t_u)
        eq = [jnp.where(take, o, e & ~p) for o, e, p in zip(ones, eq, planes)]
        gt = [jnp.where(take, g, g | o) for g, o in zip(gt, ones)]
    t = t_u ^ jnp.int32(_INT_MIN)
    n_gt = popcount(gt)
    n_ge = n_gt + popcount(eq)
    need = n_keep - n_gt
    n_bits = (key_scr.shape[0] * SPA_TK).bit_length()

    def search_last():
        def refine_pos(j, cand):
            below = count(lambda k, k0: (k == t) & ((k0 + krow) < cand))
            return jnp.where(below < need, cand, j)
        return lax.fori_loop(0, n_bits, lambda i, j: refine_pos(j, j | (jnp.int32(1) << (n_bits - 1 - i))),
                             jnp.zeros((1, SPA_TQ), jnp.int32))

    keep_all_ties = lambda: jnp.full((1, SPA_TQ), 2 ** n_bits, jnp.int32)
    last = lax.cond(jnp.max(n_ge) > n_keep, search_last, keep_all_ties)

    qs_scr[...] = stack_heads(q_ref, SPA_HEADS // 2) * (SPA_HD ** -0.5)
    m_scr[...] = jnp.full_like(m_scr, _NEG)
    acc_scr[...] = jnp.zeros_like(acc_scr)
    vrow = lax.broadcasted_iota(jnp.int32, (LANES, SPA_TK), 0)
    pair_w = 2 * SPA_TQ
    pairs = [slice(pair_w * pr, pair_w * (pr + 1)) for pr in range(SPA_HEADS // 2)]

    def chunk_bias(c):
        keys = key_scr[c]
        kpos = c * SPA_TK + krow
        chosen = ((keys > t) | ((keys == t) & (kpos <= last))) & (kpos <= qpos)
        return jnp.where(chosen, 0.0, _NEG)

    def score_chunks(c, n):
        k0 = pl.multiple_of(c * SPA_TK, SPA_TK)
        bias = jnp.concatenate([chunk_bias(c + j) for j in range(n)], axis=0)
        bias2 = jnp.concatenate([bias, bias], axis=1)
        kk = kk_ref[pl.ds(k0, n * SPA_TK), :]
        for cols in pairs:
            st = _dot_nt(kk, qs_scr[cols, :]) + bias2
            for j in range(n):
                s_scr[c + j, :, cols] = st[SPA_TK * j:SPA_TK * (j + 1)]
            m_scr[:, cols] = jnp.maximum(m_scr[:, cols], jnp.max(st, axis=0, keepdims=True))

    def score_pass(i, carry):
        score_chunks(2 * i, 2)
        return carry

    def value_pass(c, carry):
        vt = vt_ref[c]
        v1t = jnp.where(vrow < SPA_HD, vt, jnp.ones_like(vt))
        for cols in pairs:
            p = jnp.exp(s_scr[c, :, cols] - m_scr[:, cols])
            acc_scr[:, cols] += _dot(v1t, p.astype(_MX))
        return carry

    lax.fori_loop(0, n_chunks // 2, score_pass, 0)

    @pl.when(n_chunks % 2 == 1)
    def _():
        score_chunks(n_chunks - 1, 1)

    lax.fori_loop(0, n_chunks, value_pass, 0)
    for pr in range(SPA_HEADS // 2):
        a = acc_scr[:, pair_w * pr:pair_w * pr + SPA_TQ]
        b = acc_scr[:, pair_w * pr + SPA_TQ:pair_w * (pr + 1)]
        tile_t = jnp.concatenate([a[:SPA_HD] / a[SPA_HD:SPA_HD + 1], b[:SPA_HD] / b[SPA_HD:SPA_HD + 1]], axis=0)
        o_ref[:, LANES * pr:LANES * (pr + 1)] = tile_t.T.astype(o_ref.dtype)


def _sparse_attn(q, kk, vt, qi, ki, wit, batch, seq):
    n = q.shape[0]
    nq = seq // SPA_TQ
    n_keep = min(TOPK_MAX, seq // 4)
    return pl.pallas_call(
        functools.partial(_sparse_attn_kernel, n_keep=n_keep),
        grid=(batch, nq),
        in_specs=[
            pl.BlockSpec((SPA_TQ, SPA_HEADS * SPA_HD), lambda b, i: (b * nq + i, 0)),
            pl.BlockSpec((seq, LANES), lambda b, i: (b, 0)),
            pl.BlockSpec((seq // SPA_TK, LANES, SPA_TK), lambda b, i: (b, 0, 0)),
            pl.BlockSpec((SPA_TQ, IDX_HEADS * IDX_HD), lambda b, i: (b * nq + i, 0)),
            pl.BlockSpec((seq, LANES), lambda b, i: (b, 0)),
            pl.BlockSpec((IDX_HEADS, SPA_TQ), lambda b, i: (0, b * nq + i)),
        ],
        out_specs=pl.BlockSpec((SPA_TQ, SPA_HEADS * SPA_HD), lambda b, i: (b * nq + i, 0)),
        out_shape=jax.ShapeDtypeStruct((n, SPA_HEADS * SPA_HD), _MX),
        scratch_shapes=[pltpu.VMEM((seq // SPA_TK, SPA_TK, SPA_TQ), jnp.int32),
                        pltpu.VMEM((seq // SPA_TK, _WORD_BITS, 8, SPA_TQ), jnp.int32),
                        pltpu.VMEM((SPA_HEADS * SPA_TQ, LANES), _MX),
                        pltpu.VMEM((seq // SPA_TK, SPA_TK, SPA_HEADS * SPA_TQ), _F32),
                        pltpu.VMEM((1, SPA_HEADS * SPA_TQ), _F32),
                        pltpu.VMEM((LANES, SPA_HEADS * SPA_TQ), _F32)],
        compiler_params=_params(("parallel", "arbitrary")),
        name="sparse_attn",
    )(q, kk, vt, qi, ki, wit)


def _mem_kv_kernel(mem_ref, g_ref, w_ref, kg_ref, k_ref, v_ref):
    h = _row_rmsnorm(mem_ref[...], g_ref[...]).astype(_MX)
    kg = kg_ref[...]
    for hd in range(X_HEADS):
        cols = slice(X_HD * hd, X_HD * (hd + 1))
        y = _dot(h, w_ref[:, cols])
        k_ref[:, cols] = _row_rmsnorm(y, kg).astype(_MX)
    for hd in range(X_HEADS):
        cols = slice(X_HD * hd, X_HD * (hd + 1))
        v_ref[:, cols] = _dot(h, w_ref[:, D_MODEL + X_HD * hd:D_MODEL + X_HD * (hd + 1)]).astype(_MX)


def _mem_kv(mem, g, wkv, kg):
    n, m = mem.shape[0], 256
    row = pl.BlockSpec((m, D_MODEL), lambda i: (i, 0))
    return pl.pallas_call(
        _mem_kv_kernel,
        grid=(n // m,),
        in_specs=[row, _const_spec(g.shape), _const_spec(wkv.shape), _const_spec(kg.shape)],
        out_specs=[row, row],
        out_shape=[jax.ShapeDtypeStruct((n, D_MODEL), _MX)] * 2,
        compiler_params=_params(("parallel",)),
        name="mem_kv",
    )(mem, g, wkv, kg)


XATT_TM = 512


def _xattn_kernel(*refs, n_mix):
    x_ref = refs[0]
    mix_refs = refs[1:1 + n_mix]
    mixw_refs = refs[1 + n_mix:1 + 2 * n_mix]
    g_ref, wq_ref, qg_ref, k_ref, v_ref, wo_ref, out_ref, o_scr = refs[1 + 2 * n_mix:]
    x = x_ref[...]
    for a_ref, w_ref in zip(mix_refs, mixw_refs):
        x = x + _dot(a_ref[...], w_ref[...])
    h = _row_rmsnorm(x, g_ref[...]).astype(_MX)
    qg = qg_ref[...]
    scale = X_HD ** -0.5
    heads = [slice(X_HD * hd, X_HD * (hd + 1)) for hd in range(X_HEADS)]
    qs = [_dot(h, wq_ref[:, cols]) for cols in heads]
    qs = [_row_rmsnorm(q, qg).astype(_MX) for q in qs]
    ss = [_dot_nt(q, k_ref[:, cols]) * scale for q, cols in zip(qs, heads)]
    for s, cols in zip(ss, heads):
        p = jnp.exp(s - jnp.max(s, axis=-1, keepdims=True))
        p = p / jnp.sum(p, axis=-1, keepdims=True)
        o_scr[:, cols] = _dot(p.astype(_MX), v_ref[:, cols]).astype(_MX)
    out_ref[...] = x + _dot(o_scr[...], wo_ref[...])


def _xattn(x, mix, mix_w, g, wq, qg, k, v, wo, batch, seq, mem_len):
    n = x.shape[0]
    tm = XATT_TM
    nt = seq // tm
    row = lambda w: pl.BlockSpec((tm, w), lambda b, i: (b * nt + i, 0))
    const = lambda a: pl.BlockSpec(a.shape, lambda b, i: (0,) * a.ndim)
    memspec = pl.BlockSpec((mem_len, D_MODEL), lambda b, i: (b, 0))
    return pl.pallas_call(
        functools.partial(_xattn_kernel, n_mix=len(mix)),
        grid=(batch, nt),
        in_specs=([row(D_MODEL)] + [row(a.shape[1]) for a in mix] + [const(w) for w in mix_w]
                  + [const(g), const(wq), const(qg), memspec, memspec, const(wo)]),
        out_specs=row(D_MODEL),
        out_shape=jax.ShapeDtypeStruct(x.shape, _F32),
        scratch_shapes=[pltpu.VMEM((tm, D_MODEL), _MX)],
        compiler_params=_params(("parallel", "parallel")),
        name="xattn",
    )(x, *mix, *mix_w, g, wq, qg, k, v, wo)


def _seg_matrix(seg):
    idx = np.arange(MXU_WIDTH) // seg
    return jnp.asarray(idx[:, None] == idx[None, :], _MX)


def _rope_tables(positions, dim):
    inv = jnp.exp(-math.log(ROPE_THETA) * jnp.arange(0, dim, 2, dtype=_F32) / dim)
    ang = positions.astype(_F32).reshape(-1)[:, None] * inv
    return jnp.cos(ang), jnp.sin(ang)


def _tables(positions):
    cos64, sin64 = _rope_tables(positions, DIL_HD)
    cos32, sin32 = _rope_tables(positions, MLA_ROPE)
    n = cos64.shape[0]
    c64 = jnp.tile(cos64, (1, 4))
    n64 = jnp.tile(jnp.concatenate([-sin64, sin64], axis=1), (1, 2))
    ones, zeros = jnp.ones((n, MLA_NOPE), _F32), jnp.zeros((n, MLA_NOPE), _F32)
    pad1, pad0 = jnp.ones((n, 32), _F32), jnp.zeros((n, 32), _F32)
    cm = jnp.concatenate([ones, cos32, cos32, pad1], axis=1)
    nm = jnp.concatenate([zeros, -sin32, sin32, pad0], axis=1)
    return c64, n64, cm, nm


def _even_weights(w_in, q_lora_g, kv_lora_g, w_uq, w_ukv, q_g, k_g, dq_g, dk_g, w_out):
    o1, o2, o3 = MLA_Q_LORA, MLA_Q_LORA + MLA_KV_LORA, MLA_Q_LORA + MLA_KV_LORA + MLA_ROPE
    d = w_in.shape[0]
    z = lambda w: jnp.zeros((d, w), _F32)
    win = jnp.concatenate([w_in[:, :o2], z(64), w_in[:, o2:o3], z(32), w_in[:, o3:]], axis=1).astype(_MX)
    ones = lambda w: jnp.ones((w,), _F32)
    zeros = lambda w: jnp.zeros((w,), _F32)
    grp_gain = jnp.concatenate([jnp.tile(dq_g, DIL_HEADS), jnp.tile(dk_g, DIL_HEADS), ones(256)])
    pin = jnp.stack([
        jnp.concatenate([ones(256) / 256, ones(128) / 128, ones(128) / MLA_ROPE, ones(DIL_COLS) / DIL_HD]),
        jnp.concatenate([q_lora_g, kv_lora_g, zeros(64), k_g[MLA_NOPE:], zeros(32), jnp.tile(grp_gain, 3)]),
    ])
    wq = jnp.pad(w_uq.reshape(MLA_Q_LORA, MLA_HEADS, MLA_NOPE + MLA_ROPE), ((0, 0), (0, 0), (0, 32)))
    wq = wq.reshape(MLA_Q_LORA, MLA_HEADS * LANES).astype(_MX)
    pq = jnp.stack([
        jnp.tile(jnp.concatenate([ones(64) / MLA_NOPE, ones(64) / MLA_ROPE]), MLA_HEADS),
        jnp.tile(jnp.concatenate([q_g, zeros(32)]), MLA_HEADS),
    ])
    ukv = w_ukv.reshape(MLA_KV_LORA, MLA_HEADS, MLA_NOPE + MLA_V)
    wk = jnp.pad(ukv[:, :, :MLA_NOPE], ((0, 0), (0, 0), (0, 64))).reshape(MLA_KV_LORA, MLA_HEADS * LANES).astype(_MX)
    wv = ukv[:, :, MLA_NOPE:].reshape(MLA_KV_LORA, MLA_HEADS * MLA_V).astype(_MX)
    pk = jnp.stack([
        jnp.tile(ones(LANES) / MLA_NOPE, MLA_HEADS),
        jnp.tile(jnp.concatenate([k_g[:MLA_NOPE], zeros(64)]), MLA_HEADS),
    ])
    return win, wq, wk, wv, pin, pq, pk, w_out.astype(_MX)


def _odd_weights(w_in, q_g, k_g, w_out):
    o1 = SPA_HEADS * SPA_HD
    o2, o3 = o1 + SPA_HD, o1 + 2 * SPA_HD
    o4 = o3 + IDX_HEADS * IDX_HD
    o5 = o4 + IDX_HD
    d = w_in.shape[0]
    k, v, ki = w_in[:, o1:o2], w_in[:, o2:o3], w_in[:, o4:o5]
    win = jnp.concatenate([w_in[:, :o1], k, k, v, v, w_in[:, o3:o4], ki, ki, w_in[:, o5:],
                           jnp.zeros((d, LANES - IDX_HEADS), _F32)], axis=1).astype(_MX)
    ones = jnp.ones((ODD_IN_PAD,), _F32)
    gain = jnp.concatenate([jnp.tile(q_g, SPA_HEADS), k_g, k_g, jnp.ones((ODD_IN_PAD - o1 - 2 * SPA_HD,), _F32)])
    pin = jnp.stack([ones / SPA_HD, gain])
    return win, pin, w_out.astype(_MX)


def kernel(x, mem, positions, ffn1_norm, ffn1_w13, ffn1_w2, mix_norm, xattn_norm, mem_norm,
           xattn_wq, xattn_wkv, xattn_q_gain, xattn_k_gain, xattn_wo, ffn2_norm, ffn2_w13, ffn2_w2,
           even_w_in, mla_q_lora_norm, mla_kv_lora_norm, mla_w_uq, mla_w_ukv, mla_q_gain, mla_k_gain,
           dil_q_gain, dil_k_gain, even_w_out, odd_w_in, sparse_q_gain, sparse_k_gain, odd_w_out):
    batch, seq, d = x.shape
    mem_len = mem.shape[1]
    n = batch * seq
    c64, n64, cm, nm = _tables(positions)
    s64, s128, s256 = _seg_matrix(64), _seg_matrix(128), _seg_matrix(256)
    bf = lambda a: a.astype(_MX)
    row3 = lambda a: a.reshape(a.shape[0], 1, a.shape[1])
    f1w13, f1w2, f2w13, f2w2 = bf(ffn1_w13), bf(ffn1_w2), bf(ffn2_w13), bf(ffn2_w2)
    f1g, f2g = row3(ffn1_norm), row3(ffn2_norm)
    xwq, xwkv, xwo = bf(xattn_wq), bf(xattn_wkv), bf(xattn_wo)

    xs = x.reshape(n, d)
    mems = mem.reshape(batch * mem_len, d)
    for i in range(DEPTH):
        j = i // 2
        xs = _ffn(xs, f1g, f1w13, f1w2, i)
        g_mix = mix_norm[i][None, :]
        if i % 2 == 0:
            win, wq, wk, wv, pin, pq, pk, wout = _even_weights(
                even_w_in[j], mla_q_lora_norm[j], mla_kv_lora_norm[j], mla_w_uq[j], mla_w_ukv[j],
                mla_q_gain[j], mla_k_gain[j], dil_q_gain[j], dil_k_gain[j], even_w_out[j])
            qm, km, vt, dil = _even_proj(xs, g_mix, win, wq, wk, wv, pin, pq, pk,
                                         (s64, s128, s256), (c64, n64, cm, nm))
            n_mla = MLA_HEADS * MLA_V
            mix = [_mla_attn(qm, km, vt, batch, seq), _dil_attn(dil, batch, seq)]
            mix_w = [wout[:n_mla], wout[n_mla:]]
        else:
            win, pin, wout = _odd_weights(odd_w_in[j], sparse_q_gain[j], sparse_k_gain[j], odd_w_out[j])
            q, kk, vt, qi, ki, wit = _odd_proj(xs, g_mix, win, pin, s64, (c64, n64))
            mix = [_sparse_attn(q, kk, vt, qi, ki, wit, batch, seq)]
            mix_w = [wout]
        mk, mv = _mem_kv(mems, mem_norm[i][None, :], xwkv[i], xattn_k_gain[i][None, :])
        xs = _xattn(xs, mix, mix_w, xattn_norm[i][None, :], xwq[i], xattn_q_gain[i][None, :], mk, mv, xwo[i],
                    batch, seq, mem_len)
        xs = _ffn(xs, f2g, f2w13, f2w2, i)
    return xs.reshape(batch, seq, d)
```

```python
import functools
import math

import jax
import jax.numpy as jnp
import numpy as np
from jax import lax
from jax.experimental import pallas as pl
from jax.experimental.pallas import tpu as pltpu

D_MODEL = 1024
DEPTH = 4
EPS = 1e-6
D_FF = 2816
ROPE_THETA = 10000.0
MLA_HEADS, MLA_NOPE, MLA_ROPE, MLA_V = 8, 64, 32, 64
MLA_Q_LORA, MLA_KV_LORA = 256, 128
DIL_PATTERNS = ((128, 1), (512, 4), (2048, 16))
DIL_HEADS, DIL_HD = 4, 64
SPA_HEADS, SPA_HD = 16, 64
IDX_HEADS, IDX_HD = 8, 64
TOPK_MAX = 256
X_HEADS = 4
X_HD = D_MODEL // X_HEADS

LANES = 128
MXU_WIDTH = 256
VMEM_LIMIT_BYTES = 56 * 1024 * 1024

_MX = jnp.bfloat16
_F32 = jnp.float32
_NEG = -1e30


def _dot(a, b):
    return jnp.dot(a, b, preferred_element_type=_F32)


def _dot_nt(a, b):
    return lax.dot_general(a, b, (((1,), (1,)), ((), ())), preferred_element_type=_F32)


def _params(sem, flags=None):
    return pltpu.CompilerParams(dimension_semantics=sem, vmem_limit_bytes=VMEM_LIMIT_BYTES, flags=flags)


def _row_rmsnorm(x, g):
    return x * lax.rsqrt(jnp.mean(x * x, axis=-1, keepdims=True) + EPS) * g


def _seg_sumsq(y, seg_ref):
    return _dot((y * y).astype(_MX), seg_ref[...])


def _swap_lanes(z, d):
    lane = lax.broadcasted_iota(jnp.int32, z.shape, 1)
    fwd = pltpu.roll(z, LANES - d, 1)
    bwd = pltpu.roll(z, d, 1)
    return jnp.where((lane & d) == 0, fwd, bwd)


def _rope_tile(z, d, cos, sin_signed):
    return z * cos + _swap_lanes(z, d) * sin_signed


def _run_pipelined(jobs):
    y_next = jobs[0][0]()
    for i, (_, epilogue) in enumerate(jobs):
        y = y_next
        if i + 1 < len(jobs):
            y_next = jobs[i + 1][0]()
        epilogue(y)


FFN_TM = 512
FFN_CHUNKS = (768, 768, 768, 512)
assert sum(FFN_CHUNKS) == D_FF and all(c % MXU_WIDTH == 0 for c in FFN_CHUNKS)


def _ffn_kernel(x_ref, g_ref, w13_ref, w2_ref, o_ref, act_scr):
    x = x_ref[...]
    h = _row_rmsnorm(x, g_ref[...]).astype(_MX)
    c0 = 0
    for width in FFN_CHUNKS:
        gate = _dot(h, w13_ref[:, c0:c0 + width])
        up = _dot(h, w13_ref[:, D_FF + c0:D_FF + c0 + width])
        act_scr[:, c0:c0 + width] = (jax.nn.silu(gate) * up).astype(_MX)
        c0 += width
    o_ref[...] = x + 0.5 * _dot(act_scr[...], w2_ref[...])


def _ffn(x, g, w13, w2, layer):
    n = x.shape[0]
    resident = pl.Buffered(1)
    return pl.pallas_call(
        _ffn_kernel,
        grid=(n // FFN_TM,),
        in_specs=[
            pl.BlockSpec((FFN_TM, D_MODEL), lambda i: (i, 0)),
            pl.BlockSpec((None, 1, D_MODEL), lambda i: (layer, 0, 0)),
            pl.BlockSpec((None, D_MODEL, 2 * D_FF), lambda i: (layer, 0, 0), pipeline_mode=resident),
            pl.BlockSpec((None, D_FF, D_MODEL), lambda i: (layer, 0, 0), pipeline_mode=resident),
        ],
        out_specs=pl.BlockSpec((FFN_TM, D_MODEL), lambda i: (i, 0)),
        out_shape=jax.ShapeDtypeStruct(x.shape, _F32),
        scratch_shapes=[pltpu.VMEM((FFN_TM, D_FF), _MX)],
        compiler_params=_params(("parallel",)),
        name="ffn",
    )(x, g, w13, w2)


PROJ_TM = 256
EVEN_IN_PAD = 2816
DIL_COLS = 2304


def _even_proj_kernel(x_ref, g_ref, win_ref, wq_ref, wk_ref, wv_ref, pin_ref, pq_ref, pk_ref,
                      s64_ref, s128_ref, s256_ref, c64_ref, n64_ref, cm_ref, nm_ref,
                      qm_ref, km_ref, vt_ref, dil_ref):
    h = _row_rmsnorm(x_ref[...], g_ref[...]).astype(_MX)
    c64, n64, cm, nm = c64_ref[...], n64_ref[...], cm_ref[...], nm_ref[...]

    def normed(y, seg_ref, p_ref, c0):
        ss = _seg_sumsq(y, seg_ref)
        return y * lax.rsqrt(ss * p_ref[0:1, c0:c0 + MXU_WIDTH] + EPS) * p_ref[1:2, c0:c0 + MXU_WIDTH]

    tiles = lambda c0: [(t, slice(c0 + LANES * t, c0 + LANES * (t + 1))) for t in range(2)]
    latent = {}
    jobs = []

    def epi_cq(y):
        latent["cq"] = normed(y, s256_ref, pin_ref, 0).astype(_MX)
    jobs.append((lambda: _dot(h, win_ref[:, 0:256]), epi_cq))

    def epi_ckv(y):
        z = normed(y, s128_ref, pin_ref, 256)
        latent["ckv"] = z[:, :LANES].astype(_MX)
        latent["k_rope"] = _rope_tile(z[:, LANES:], MLA_ROPE // 2, cm, nm)
    jobs.append((lambda: _dot(h, win_ref[:, 256:512]), epi_ckv))

    for j in range(DIL_COLS // MXU_WIDTH):
        def epi_dil(y, j=j):
            if j % 3 < 2:
                z = normed(y, s64_ref, pin_ref, 512 + MXU_WIDTH * j)
                for t, cols in tiles(MXU_WIDTH * j):
                    dil_ref[:, cols] = _rope_tile(z[:, LANES * t:LANES * (t + 1)], DIL_HD // 2, c64, n64)
            else:
                dil_ref[:, MXU_WIDTH * j:MXU_WIDTH * (j + 1)] = y
        jobs.append((lambda j=j: _dot(h, win_ref[:, 512 + MXU_WIDTH * j:512 + MXU_WIDTH * (j + 1)]), epi_dil))

    for j in range(MLA_HEADS * LANES // MXU_WIDTH):
        c0 = MXU_WIDTH * j
        def epi_q(y, c0=c0):
            z = normed(y, s64_ref, pq_ref, c0)
            for t, cols in tiles(c0):
                qm_ref[:, cols] = _rope_tile(z[:, LANES * t:LANES * (t + 1)], MLA_ROPE // 2, cm, nm).astype(_MX)
        jobs.append((lambda c0=c0: _dot(latent["cq"], wq_ref[:, c0:c0 + MXU_WIDTH]), epi_q))

        def epi_k(y, c0=c0):
            z = normed(y, s64_ref, pk_ref, c0)
            for t, cols in tiles(c0):
                km_ref[:, cols] = (z[:, LANES * t:LANES * (t + 1)] + latent["k_rope"]).astype(_MX)
        jobs.append((lambda c0=c0: _dot(latent["ckv"], wk_ref[:, c0:c0 + MXU_WIDTH]), epi_k))

    for j in range(MLA_HEADS * MLA_V // MXU_WIDTH):
        c0 = MXU_WIDTH * j
        def epi_v(y, c0=c0):
            for t, cols in tiles(c0):
                vt_ref[cols, :] = y[:, LANES * t:LANES * (t + 1)].T.astype(_MX)
        jobs.append((lambda c0=c0: _dot(latent["ckv"], wv_ref[:, c0:c0 + MXU_WIDTH]), epi_v))
    _run_pipelined(jobs)


def _const_spec(shape):
    nd = len(shape)
    return pl.BlockSpec(shape, lambda i: (0,) * nd)


def _even_proj(x, g, win, wq, wk, wv, pin, pq, pk, segs, tabs):
    n = x.shape[0]
    tm = PROJ_TM
    row = lambda w: pl.BlockSpec((tm, w), lambda i: (i, 0))
    s64, s128, s256 = segs
    c64, n64, cm, nm = tabs
    return pl.pallas_call(
        _even_proj_kernel,
        grid=(n // tm,),
        in_specs=[row(D_MODEL), _const_spec(g.shape), _const_spec(win.shape), _const_spec(wq.shape),
                  _const_spec(wk.shape), _const_spec(wv.shape), _const_spec(pin.shape),
                  _const_spec(pq.shape), _const_spec(pk.shape), _const_spec(s64.shape),
                  _const_spec(s128.shape), _const_spec(s256.shape),
                  row(LANES), row(LANES), row(LANES), row(LANES)],
        out_specs=[row(MLA_HEADS * LANES), row(MLA_HEADS * LANES),
                   pl.BlockSpec((None, MLA_HEADS * MLA_V, tm), lambda i: (i, 0, 0)), row(DIL_COLS)],
        out_shape=[jax.ShapeDtypeStruct((n, MLA_HEADS * LANES), _MX),
                   jax.ShapeDtypeStruct((n, MLA_HEADS * LANES), _MX),
                   jax.ShapeDtypeStruct((n // tm, MLA_HEADS * MLA_V, tm), _MX),
                   jax.ShapeDtypeStruct((n, DIL_COLS), _F32)],
        compiler_params=_params(("parallel",)),
        name="even_proj",
    )(x, g, win, wq, wk, wv, pin, pq, pk, s64, s128, s256, c64, n64, cm, nm)


MLA_TQ = 256
MLA_TK = 256


def _mla_attn_kernel(q_ref, k_ref, vt_ref, o_ref, s_scr, m_scr, acc_scr):
    qi = pl.program_id(1)
    scale = (MLA_NOPE + MLA_ROPE) ** -0.5
    m_scr[...] = jnp.full_like(m_scr, _NEG)
    acc_scr[...] = jnp.zeros_like(acc_scr)
    kpos = lax.broadcasted_iota(jnp.int32, (MLA_TK, MLA_TQ), 0)
    qcol = lax.broadcasted_iota(jnp.int32, (MLA_TK, MLA_TQ), 1)

    def score_chunks(c, n, diagonal):
        k0 = pl.multiple_of(c * MLA_TK, MLA_TK)
        for hd in range(MLA_HEADS):
            cols = slice(LANES * hd, LANES * (hd + 1))
            st = _dot_nt(k_ref[pl.ds(k0, n * MLA_TK), cols], q_ref[:, cols])
            if diagonal:
                st = jnp.where(kpos <= qcol, st, _NEG)
            for j in range(n):
                s_scr[c + j, hd] = st[MLA_TK * j:MLA_TK * (j + 1)]
            m_scr[hd] = jnp.maximum(m_scr[hd], jnp.max(st, axis=0, keepdims=True))

    def score_pass(i, carry):
        score_chunks(2 * i, 2, False)
        return carry

    ones = jnp.ones((MLA_V, MLA_TK), _MX)

    def value_pass(c, carry):
        for hd in range(MLA_HEADS):
            p = jnp.exp2((s_scr[c, hd] - m_scr[hd]) * (scale * math.log2(math.e)))
            v1t = jnp.concatenate([vt_ref[c, MLA_V * hd:MLA_V * (hd + 1), :], ones], axis=0)
            acc_scr[hd] += _dot(v1t, p.astype(_MX))
        return carry

    lax.fori_loop(0, qi // 2, score_pass, 0)

    @pl.when(qi % 2 == 1)
    def _():
        score_chunks(qi - 1, 1, False)

    score_chunks(qi, 1, True)
    lax.fori_loop(0, qi + 1, value_pass, 0)
    ot = jnp.concatenate([acc_scr[hd, :MLA_V] / acc_scr[hd, MLA_V:MLA_V + 1] for hd in range(MLA_HEADS)],
                         axis=0)
    o_ref[...] = ot.T.astype(o_ref.dtype)


def _mla_attn(qm, km, vt, batch, seq):
    n = qm.shape[0]
    nq = seq // MLA_TQ
    wv = MLA_HEADS * MLA_V
    return pl.pallas_call(
        _mla_attn_kernel,
        grid=(batch, nq),
        in_specs=[
            pl.BlockSpec((MLA_TQ, MLA_HEADS * LANES), lambda b, i: (b * nq + i, 0)),
            pl.BlockSpec((seq, MLA_HEADS * LANES), lambda b, i: (b, 0)),
            pl.BlockSpec((seq // MLA_TK, wv, MLA_TK), lambda b, i: (b, 0, 0)),
        ],
        out_specs=pl.BlockSpec((MLA_TQ, wv), lambda b, i: (b * nq + i, 0)),
        out_shape=jax.ShapeDtypeStruct((n, wv), _MX),
        scratch_shapes=[pltpu.VMEM((seq // MLA_TK, MLA_HEADS, MLA_TK, MLA_TQ), _F32),
                        pltpu.VMEM((MLA_HEADS, 1, MLA_TQ), _F32),
                        pltpu.VMEM((MLA_HEADS, 2 * MLA_V, MLA_TQ), _F32)],
        compiler_params=_params(("parallel", "arbitrary")),
        name="mla_attn",
    )(qm, km, vt)


DIL_BLK = 128
DIL_UNROLL = 8


def _dil_attn_kernel(*refs, seq):
    n_grp = len(DIL_PATTERNS)
    qkv = [refs[3 * g:3 * g + 3] for g in range(n_grp)]
    out_ref = refs[3 * n_grp]
    o_slabs = refs[3 * n_grp + 1:3 * n_grp + 1 + n_grp]
    l_slabs = refs[3 * n_grp + 1 + n_grp:]
    scale = DIL_HD ** -0.5
    kidx = lax.broadcasted_iota(jnp.int32, (DIL_BLK, DIL_BLK), 0)
    qidx = lax.broadcasted_iota(jnp.int32, (DIL_BLK, DIL_BLK), 1)
    lane = lax.broadcasted_iota(jnp.int32, (DIL_BLK, LANES), 1)
    dim = lax.broadcasted_iota(jnp.int32, (LANES, DIL_BLK), 0)
    cur_ok = kidx <= qidx

    for g, (window, d) in enumerate(DIL_PATTERNS):
        q_ref, k_ref, v_ref = qkv[g]
        nb = seq // d // DIL_BLK

        def units(it, carry, d=d, nb=nb, q_ref=q_ref, k_ref=k_ref, v_ref=v_ref, g=g):
            staged = []
            for u in range(DIL_UNROLL):
                idx = it * DIL_UNROLL + u
                r = idx // nb
                n = idx % nb
                rows = pl.ds(r + d * DIL_BLK * n, DIL_BLK, stride=d)
                prev = pl.ds(r + d * DIL_BLK * jnp.maximum(n - 1, 0), DIL_BLK, stride=d)
                prev_ok = kidx >= qidx + jnp.where(n > 0, 0, DIL_BLK)
                q2 = q_ref[rows, :] * scale
                kc = k_ref[rows, :].astype(_MX)
                kp = k_ref[prev, :].astype(_MX)
                halves = []
                for half in range(2):
                    mine = (lane < DIL_HD) if half == 0 else (lane >= DIL_HD)
                    qh = jnp.where(mine, q2, 0.0).astype(_MX)
                    halves.append((_dot_nt(kc, qh), _dot_nt(kp, qh)))
                staged.append((rows, prev, prev_ok, halves))
            for rows, prev, prev_ok, halves in staged:
                vct = v_ref[rows, :].T.astype(_MX)
                vpt = v_ref[prev, :].T.astype(_MX)
                ots, lses = [], []
                for sc, sp in halves:
                    sc = jnp.where(cur_ok, sc, _NEG)
                    sp = jnp.where(prev_ok, sp, _NEG)
                    m = jnp.maximum(jnp.max(sc, axis=0, keepdims=True), jnp.max(sp, axis=0, keepdims=True))
                    pc = jnp.exp(sc - m)
                    pp = jnp.exp(sp - m)
                    l = jnp.sum(pc, axis=0, keepdims=True) + jnp.sum(pp, axis=0, keepdims=True)
                    ots.append((_dot(vct, pc.astype(_MX)) + _dot(vpt, pp.astype(_MX))) / l)
                    lses.append(m + jnp.log(l))
                ot = jnp.where(dim < DIL_HD, ots[0], ots[1])
                lt = jnp.where(dim < DIL_HD, lses[0], lses[1])
                o_slabs[g][rows, :] = ot.T
                l_slabs[g][rows, :] = lt.T
            return carry

        lax.fori_loop(0, d * nb // DIL_UNROLL, units, 0)

    ls = [l_slabs[g][...] for g in range(n_grp)]
    m = functools.reduce(jnp.maximum, ls)
    es = [jnp.exp(l - m) for l in ls]
    num = sum(e * o_slabs[g][...] for g, e in enumerate(es))
    out_ref[...] = (num / sum(es)).astype(out_ref.dtype)


def _dil_attn(dil, batch, seq):
    n = dil.shape[0]
    pairs = DIL_HEADS // 2
    spec = lambda g, part: pl.BlockSpec((seq, LANES), lambda b, p: (b, (3 * g + part) * pairs + p))
    in_specs = [spec(g, part) for g in range(len(DIL_PATTERNS)) for part in range(3)]
    return pl.pallas_call(
        functools.partial(_dil_attn_kernel, seq=seq),
        grid=(batch, pairs),
        in_specs=in_specs,
        out_specs=pl.BlockSpec((seq, LANES), lambda b, p: (b, p)),
        out_shape=jax.ShapeDtypeStruct((n, DIL_HEADS * DIL_HD), _MX),
        scratch_shapes=[pltpu.VMEM((seq, LANES), _F32)] * (2 * len(DIL_PATTERNS)),
        compiler_params=_params(("parallel", "parallel")),
        name="dil_attn",
    )(*([dil] * len(in_specs)))


ODD_IN_PAD = 2048


def _odd_proj_kernel(x_ref, g_ref, win_ref, pin_ref, s64_ref, c64_ref, n64_ref,
                     q_ref, kk_ref, vt_ref, qi_ref, ki_ref, wit_ref):
    h = _row_rmsnorm(x_ref[...], g_ref[...]).astype(_MX)
    c64, n64 = c64_ref[...], n64_ref[...]

    def normed(y, c0):
        ss = _seg_sumsq(y, s64_ref)
        return y * lax.rsqrt(ss * pin_ref[0:1, c0:c0 + MXU_WIDTH] + EPS) * pin_ref[1:2, c0:c0 + MXU_WIDTH]

    def rope64(t):
        return _rope_tile(t, SPA_HD // 2, c64, n64)

    nq = SPA_HEADS * SPA_HD
    c_qi = nq + MXU_WIDTH
    c_ki = c_qi + IDX_HEADS * IDX_HD
    product = lambda c0: (lambda: _dot(h, win_ref[:, c0:c0 + MXU_WIDTH]))
    jobs = []
    for j in range(nq // MXU_WIDTH):
        def epi_q(y, c0=MXU_WIDTH * j):
            z = normed(y, c0)
            for t in range(2):
                q_ref[:, c0 + LANES * t:c0 + LANES * (t + 1)] = rope64(z[:, LANES * t:LANES * (t + 1)]).astype(_MX)
        jobs.append((product(MXU_WIDTH * j), epi_q))

    def epi_kv(y):
        z = normed(y, nq)
        kk_ref[...] = rope64(z[:, :LANES]).astype(_MX)
        vt_ref[...] = y[:, LANES:].T.astype(_MX)
    jobs.append((product(nq), epi_kv))

    for j in range(IDX_HEADS * IDX_HD // MXU_WIDTH):
        def epi_qi(y, o0=MXU_WIDTH * j):
            for t in range(2):
                qi_ref[:, o0 + LANES * t:o0 + LANES * (t + 1)] = rope64(y[:, LANES * t:LANES * (t + 1)]).astype(_MX)
        jobs.append((product(c_qi + MXU_WIDTH * j), epi_qi))

    def epi_ki(y):
        ki_ref[...] = rope64(y[:, :LANES]).astype(_MX)
        wi = (y[:, LANES:] * (IDX_HEADS ** -0.5)) * (IDX_HD ** -0.5)
        wit_ref[...] = wi.T[0:IDX_HEADS, :]
    jobs.append((product(c_ki), epi_ki))
    _run_pipelined(jobs)


def _odd_proj(x, g, win, pin, s64, tabs):
    n = x.shape[0]
    tm = PROJ_TM
    row = lambda w: pl.BlockSpec((tm, w), lambda i: (i, 0))
    c64, n64 = tabs
    return pl.pallas_call(
        _odd_proj_kernel,
        grid=(n // tm,),
        in_specs=[row(D_MODEL), _const_spec(g.shape), _const_spec(win.shape), _const_spec(pin.shape),
                  _const_spec(s64.shape), row(LANES), row(LANES)],
        out_specs=[row(SPA_HEADS * SPA_HD), row(LANES),
                   pl.BlockSpec((None, LANES, tm), lambda i: (i, 0, 0)),
                   row(IDX_HEADS * IDX_HD), row(LANES),
                   pl.BlockSpec((IDX_HEADS, tm), lambda i: (0, i))],
        out_shape=[jax.ShapeDtypeStruct((n, SPA_HEADS * SPA_HD), _MX),
                   jax.ShapeDtypeStruct((n, LANES), _MX),
                   jax.ShapeDtypeStruct((n // tm, LANES, tm), _MX),
                   jax.ShapeDtypeStruct((n, IDX_HEADS * IDX_HD), _MX),
                   jax.ShapeDtypeStruct((n, LANES), _MX),
                   jax.ShapeDtypeStruct((IDX_HEADS, n), _F32)],
        compiler_params=_params(("parallel",)),
        name="odd_proj",
    )(x, g, win, pin, s64, c64, n64)


SPA_TQ = 128
SPA_TK = 256
_INT_MIN = -2 ** 31
assert SPA_TK == PROJ_TM and MLA_TK == PROJ_TM and MLA_TQ == MLA_TK


_WORD_BITS = 32
assert SPA_TK == 8 * _WORD_BITS


def _bit_transpose32(words):
    words = list(words)
    for shift, mask in ((16, 0x0000FFFF), (8, 0x00FF00FF), (4, 0x0F0F0F0F), (2, 0x33333333), (1, 0x55555555)):
        for k in range(_WORD_BITS):
            if k & shift == 0:
                t = (lax.shift_right_logical(words[k], jnp.int32(shift)) ^ words[k + shift]) & jnp.int32(mask)
                words[k] = words[k] ^ (t << shift)
                words[k + shift] = words[k + shift] ^ t
    return words


def _sparse_attn_kernel(q_ref, kk_ref, vt_ref, qi_ref, ki_ref, wit_ref, o_ref,
                        key_scr, plane_scr, qs_scr, s_scr, m_scr, acc_scr, *, n_keep):
    qb = pl.program_id(1)
    n_chunks = qb // (SPA_TK // SPA_TQ) + 1
    lane = lax.broadcasted_iota(jnp.int32, (SPA_TQ, LANES), 1)
    left = lane < SPA_HD

    def stack_heads(ref, n_pairs):
        parts = []
        for pr in range(n_pairs):
            t = ref[:, LANES * pr:LANES * (pr + 1)]
            parts.append(jnp.where(left, t, jnp.zeros_like(t)))
            parts.append(jnp.where(left, jnp.zeros_like(t), t))
        return jnp.concatenate(parts, axis=0)

    qi_stack = stack_heads(qi_ref, IDX_HEADS // 2)
    wit = wit_ref[...]
    krow = lax.broadcasted_iota(jnp.int32, (SPA_TK, SPA_TQ), 0)
    qpos = qb * SPA_TQ + lax.broadcasted_iota(jnp.int32, (SPA_TK, SPA_TQ), 1)

    def score_chunk(c, carry):
        k0 = pl.multiple_of(c * SPA_TK, SPA_TK)
        logits = _dot_nt(ki_ref[pl.ds(k0, SPA_TK), :], qi_stack)
        sc = jnp.zeros((SPA_TK, SPA_TQ), _F32)
        for hd in range(IDX_HEADS):
            sc = sc + wit[hd:hd + 1, :] * jnp.maximum(logits[:, SPA_TQ * hd:SPA_TQ * (hd + 1)], 0.0)
        sc = jnp.where((k0 + krow) <= qpos, sc, -jnp.inf)
        bits = pltpu.bitcast(sc, jnp.int32)
        keys = bits ^ ((bits >> 31) & jnp.int32(0x7FFFFFFF))
        key_scr[c] = keys
        words = (keys ^ jnp.int32(_INT_MIN)).reshape(_WORD_BITS, 8, SPA_TQ)
        for i, plane in enumerate(_bit_transpose32([words[v] for v in range(_WORD_BITS)])):
            plane_scr[c, i] = plane
        return carry

    def over_chunks(body, init):
        def two(i, carry):
            return body(2 * i + 1, body(2 * i, carry))
        carry = lax.fori_loop(0, n_chunks // 2, two, init)
        return lax.cond(n_chunks % 2 == 1, lambda cr: body(n_chunks - 1, cr), lambda cr: cr, carry)

    over_chunks(score_chunk, 0)

    def count(pred):
        def body(c, acc):
            hit = pred(key_scr[c], c * SPA_TK)
            return acc + jnp.sum(hit.astype(jnp.int32).reshape(SPA_TK // 8, 8, SPA_TQ), axis=0)
        acc = over_chunks(body, jnp.zeros((8, SPA_TQ), jnp.int32))
        return jnp.sum(acc, axis=0, keepdims=True)

    n_slots = key_scr.shape[0]

    def clear_planes(c, carry):
        plane_scr[c] = jnp.zeros(plane_scr.shape[1:], jnp.int32)
        return carry

    lax.fori_loop(n_chunks, n_slots, clear_planes, 0)
    eq = [jnp.full((8, SPA_TQ), jnp.where(c < n_chunks, -1, 0), jnp.int32) for c in range(n_slots)]
    gt = [jnp.zeros((8, SPA_TQ), jnp.int32) for _ in range(n_slots)]
    popcount = lambda words: jnp.sum(sum(lax.population_count(w) for w in words), axis=0, keepdims=True)
    bit = lambda i: jnp.int32(_INT_MIN if i == _WORD_BITS - 1 else 1 << i)
    t_u = jnp.zeros((1, SPA_TQ), jnp.int32)
    for i in reversed(range(1, _WORD_BITS, 2)):
        hi = [plane_scr[c, i] for c in range(n_slots)]
        lo = [plane_scr[c, i - 1] for c in range(n_slots)]
        ones_hi = [e & p for e, p in zip(eq, hi)]
        eq_1 = ones_hi
        gt_1 = gt
        eq_0 = [e & ~p for e, p in zip(eq, hi)]
        gt_0 = [g | o for g, o in zip(gt, ones_hi)]
        take_hi = popcount(gt_0) >= n_keep
        ones_lo_1 = [e & p for e, p in zip(eq_1, lo)]
        ones_lo_0 = [e & p for e, p in zip(eq_0, lo)]
        take_lo = jnp.where(take_hi,
                            popcount([g | o for g, o in zip(gt_1, ones_lo_1)]),
                            popcount([g | o for g, o in zip(gt_0, ones_lo_0)])) >= n_keep
        t_u = jnp.where(take_hi, t_u | bit(i), t_u)
        t_u = jnp.where(take_lo, t_u | bit(i - 1), t_u)
        eq_m = [jnp.where(take_hi, a, b) for a, b in zip(eq_1, eq_0)]
        gt_m = [jnp.where(take_hi, a, b) for a, b in zip(gt_1, gt_0)]
        ones_lo = [jnp.where(take_hi, a, b) for a, b in zip(ones_lo_1, ones_lo_0)]
        eq = [jnp.where(take_lo, o, e & ~p) for o, e, p in zip(ones_lo, eq_m, lo)]
        gt = [jnp.where(take_lo, g, g | o) for g, o in zip(gt_m, ones_lo)]
    t = t_u ^ jnp.int32(_INT_MIN)
    n_gt = popcount(gt)
    n_ge = n_gt + popcount(eq)
    need = n_keep - n_gt
    n_bits = (key_scr.shape[0] * SPA_TK).bit_length()

    def search_last():
        def refine_pos(j, cand):
            below = count(lambda k, k0: (k == t) & ((k0 + krow) < cand))
            return jnp.where(below < need, cand, j)
        return lax.fori_loop(0, n_bits, lambda i, j: refine_pos(j, j | (jnp.int32(1) << (n_bits - 1 - i))),
                             jnp.zeros((1, SPA_TQ), jnp.int32))

    keep_all_ties = lambda: jnp.full((1, SPA_TQ), 2 ** n_bits, jnp.int32)
    last = lax.cond(jnp.max(n_ge) > n_keep, search_last, keep_all_ties)

    qs_scr[...] = stack_heads(q_ref, SPA_HEADS // 2) * (SPA_HD ** -0.5)
    m_scr[...] = jnp.full_like(m_scr, _NEG)
    acc_scr[...] = jnp.zeros_like(acc_scr)
    vrow = lax.broadcasted_iota(jnp.int32, (LANES, SPA_TK), 0)
    pair_w = 2 * SPA_TQ
    pairs = [slice(pair_w * pr, pair_w * (pr + 1)) for pr in range(SPA_HEADS // 2)]

    def chunk_bias(c):
        keys = key_scr[c]
        kpos = c * SPA_TK + krow
        chosen = ((keys > t) | ((keys == t) & (kpos <= last))) & (kpos <= qpos)
        return jnp.where(chosen, 0.0, _NEG)

    def score_chunks(c, n):
        k0 = pl.multiple_of(c * SPA_TK, SPA_TK)
        bias = jnp.concatenate([chunk_bias(c + j) for j in range(n)], axis=0)
        bias2 = jnp.concatenate([bias, bias], axis=1)
        kk = kk_ref[pl.ds(k0, n * SPA_TK), :]
        for cols in pairs:
            st = _dot_nt(kk, qs_scr[cols, :]) + bias2
            for j in range(n):
                s_scr[c + j, :, cols] = st[SPA_TK * j:SPA_TK * (j + 1)]
            m_scr[:, cols] = jnp.maximum(m_scr[:, cols], jnp.max(st, axis=0, keepdims=True))

    def score_pass(i, carry):
        score_chunks(2 * i, 2)
        return carry

    def value_pass(c, carry):
        vt = vt_ref[c]
        v1t = jnp.where(vrow < SPA_HD, vt, jnp.ones_like(vt))
        for cols in pairs:
            p = jnp.exp(s_scr[c, :, cols] - m_scr[:, cols])
            acc_scr[:, cols] += _dot(v1t, p.astype(_MX))
        return carry

    lax.fori_loop(0, n_chunks // 2, score_pass, 0)

    @pl.when(n_chunks % 2 == 1)
    def _():
        score_chunks(n_chunks - 1, 1)

    lax.fori_loop(0, n_chunks, value_pass, 0)
    for pr in range(SPA_HEADS // 2):
        a = acc_scr[:, pair_w * pr:pair_w * pr + SPA_TQ]
        b = acc_scr[:, pair_w * pr + SPA_TQ:pair_w * (pr + 1)]
        tile_t = jnp.concatenate([a[:SPA_HD] / a[SPA_HD:SPA_HD + 1], b[:SPA_HD] / b[SPA_HD:SPA_HD + 1]], axis=0)
        o_ref[:, LANES * pr:LANES * (pr + 1)] = tile_t.T.astype(o_ref.dtype)


def _sparse_attn(q, kk, vt, qi, ki, wit, batch, seq):
    n = q.shape[0]
    nq = seq // SPA_TQ
    n_keep = min(TOPK_MAX, seq // 4)
    return pl.pallas_call(
        functools.partial(_sparse_attn_kernel, n_keep=n_keep),
        grid=(batch, nq),
        in_specs=[
            pl.BlockSpec((SPA_TQ, SPA_HEADS * SPA_HD), lambda b, i: (b * nq + i, 0)),
            pl.BlockSpec((seq, LANES), lambda b, i: (b, 0)),
            pl.BlockSpec((seq // SPA_TK, LANES, SPA_TK), lambda b, i: (b, 0, 0)),
            pl.BlockSpec((SPA_TQ, IDX_HEADS * IDX_HD), lambda b, i: (b * nq + i, 0)),
            pl.BlockSpec((seq, LANES), lambda b, i: (b, 0)),
            pl.BlockSpec((IDX_HEADS, SPA_TQ), lambda b, i: (0, b * nq + i)),
        ],
        out_specs=pl.BlockSpec((SPA_TQ, SPA_HEADS * SPA_HD), lambda b, i: (b * nq + i, 0)),
        out_shape=jax.ShapeDtypeStruct((n, SPA_HEADS * SPA_HD), _MX),
        scratch_shapes=[pltpu.VMEM((seq // SPA_TK, SPA_TK, SPA_TQ), jnp.int32),
                        pltpu.VMEM((seq // SPA_TK, _WORD_BITS, 8, SPA_TQ), jnp.int32),
                        pltpu.VMEM((SPA_HEADS * SPA_TQ, LANES), _MX),
                        pltpu.VMEM((seq // SPA_TK, SPA_TK, SPA_HEADS * SPA_TQ), _F32),
                        pltpu.VMEM((1, SPA_HEADS * SPA_TQ), _F32),
                        pltpu.VMEM((LANES, SPA_HEADS * SPA_TQ), _F32)],
        compiler_params=_params(("parallel", "arbitrary")),
        name="sparse_attn",
    )(q, kk, vt, qi, ki, wit)


def _mem_kv_kernel(mem_ref, g_ref, w_ref, kg_ref, k_ref, v_ref):
    h = _row_rmsnorm(mem_ref[...], g_ref[...]).astype(_MX)
    kg = kg_ref[...]
    for hd in range(X_HEADS):
        cols = slice(X_HD * hd, X_HD * (hd + 1))
        y = _dot(h, w_ref[:, cols])
        k_ref[:, cols] = _row_rmsnorm(y, kg).astype(_MX)
    for hd in range(X_HEADS):
        cols = slice(X_HD * hd, X_HD * (hd + 1))
        v_ref[:, cols] = _dot(h, w_ref[:, D_MODEL + X_HD * hd:D_MODEL + X_HD * (hd + 1)]).astype(_MX)


def _mem_kv(mem, g, wkv, kg):
    n, m = mem.shape[0], 256
    row = pl.BlockSpec((m, D_MODEL), lambda i: (i, 0))
    return pl.pallas_call(
        _mem_kv_kernel,
        grid=(n // m,),
        in_specs=[row, _const_spec(g.shape), _const_spec(wkv.shape), _const_spec(kg.shape)],
        out_specs=[row, row],
        out_shape=[jax.ShapeDtypeStruct((n, D_MODEL), _MX)] * 2,
        compiler_params=_params(("parallel",)),
        name="mem_kv",
    )(mem, g, wkv, kg)


XATT_TM = 512


def _xattn_kernel(*refs, n_mix):
    x_ref = refs[0]
    mix_refs = refs[1:1 + n_mix]
    mixw_refs = refs[1 + n_mix:1 + 2 * n_mix]
    g_ref, wq_ref, qg_ref, k_ref, v_ref, wo_ref, out_ref, o_scr = refs[1 + 2 * n_mix:]
    x = x_ref[...]
    for a_ref, w_ref in zip(mix_refs, mixw_refs):
        x = x + _dot(a_ref[...], w_ref[...])
    h = _row_rmsnorm(x, g_ref[...]).astype(_MX)
    qg = qg_ref[...]
    scale = X_HD ** -0.5
    heads = [slice(X_HD * hd, X_HD * (hd + 1)) for hd in range(X_HEADS)]
    qs = [_dot(h, wq_ref[:, cols]) for cols in heads]
    qs = [_row_rmsnorm(q, qg).astype(_MX) for q in qs]
    ss = [_dot_nt(q, k_ref[:, cols]) * scale for q, cols in zip(qs, heads)]
    for s, cols in zip(ss, heads):
        p = jnp.exp(s - jnp.max(s, axis=-1, keepdims=True))
        p = p / jnp.sum(p, axis=-1, keepdims=True)
        o_scr[:, cols] = _dot(p.astype(_MX), v_ref[:, cols]).astype(_MX)
    out_ref[...] = x + _dot(o_scr[...], wo_ref[...])


def _xattn(x, mix, mix_w, g, wq, qg, k, v, wo, batch, seq, mem_len):
    n = x.shape[0]
    tm = XATT_TM
    nt = seq // tm
    row = lambda w: pl.BlockSpec((tm, w), lambda b, i: (b * nt + i, 0))
    const = lambda a: pl.BlockSpec(a.shape, lambda b, i: (0,) * a.ndim)
    memspec = pl.BlockSpec((mem_len, D_MODEL), lambda b, i: (b, 0))
    return pl.pallas_call(
        functools.partial(_xattn_kernel, n_mix=len(mix)),
        grid=(batch, nt),
        in_specs=([row(D_MODEL)] + [row(a.shape[1]) for a in mix] + [const(w) for w in mix_w]
                  + [const(g), const(wq), const(qg), memspec, memspec, const(wo)]),
        out_specs=row(D_MODEL),
        out_shape=jax.ShapeDtypeStruct(x.shape, _F32),
        scratch_shapes=[pltpu.VMEM((tm, D_MODEL), _MX)],
        compiler_params=_params(("parallel", "parallel")),
        name="xattn",
    )(x, *mix, *mix_w, g, wq, qg, k, v, wo)


def _seg_matrix(seg):
    idx = np.arange(MXU_WIDTH) // seg
    return jnp.asarray(idx[:, None] == idx[None, :], _MX)


def _rope_tables(positions, dim):
    inv = jnp.exp(-math.log(ROPE_THETA) * jnp.arange(0, dim, 2, dtype=_F32) / dim)
    ang = positions.astype(_F32).reshape(-1)[:, None] * inv
    return jnp.cos(ang), jnp.sin(ang)


def _tables(positions):
    cos64, sin64 = _rope_tables(positions, DIL_HD)
    cos32, sin32 = _rope_tables(positions, MLA_ROPE)
    n = cos64.shape[0]
    c64 = jnp.tile(cos64, (1, 4))
    n64 = jnp.tile(jnp.concatenate([-sin64, sin64], axis=1), (1, 2))
    ones, zeros = jnp.ones((n, MLA_NOPE), _F32), jnp.zeros((n, MLA_NOPE), _F32)
    pad1, pad0 = jnp.ones((n, 32), _F32), jnp.zeros((n, 32), _F32)
    cm = jnp.concatenate([ones, cos32, cos32, pad1], axis=1)
    nm = jnp.concatenate([zeros, -sin32, sin32, pad0], axis=1)
    return c64, n64, cm, nm


def _even_weights(w_in, q_lora_g, kv_lora_g, w_uq, w_ukv, q_g, k_g, dq_g, dk_g, w_out):
    o1, o2, o3 = MLA_Q_LORA, MLA_Q_LORA + MLA_KV_LORA, MLA_Q_LORA + MLA_KV_LORA + MLA_ROPE
    d = w_in.shape[0]
    z = lambda w: jnp.zeros((d, w), _F32)
    win = jnp.concatenate([w_in[:, :o2], z(64), w_in[:, o2:o3], z(32), w_in[:, o3:]], axis=1).astype(_MX)
    ones = lambda w: jnp.ones((w,), _F32)
    zeros = lambda w: jnp.zeros((w,), _F32)
    grp_gain = jnp.concatenate([jnp.tile(dq_g, DIL_HEADS), jnp.tile(dk_g, DIL_HEADS), ones(256)])
    pin = jnp.stack([
        jnp.concatenate([ones(256) / 256, ones(128) / 128, ones(128) / MLA_ROPE, ones(DIL_COLS) / DIL_HD]),
        jnp.concatenate([q_lora_g, kv_lora_g, zeros(64), k_g[MLA_NOPE:], zeros(32), jnp.tile(grp_gain, 3)]),
    ])
    wq = jnp.pad(w_uq.reshape(MLA_Q_LORA, MLA_HEADS, MLA_NOPE + MLA_ROPE), ((0, 0), (0, 0), (0, 32)))
    wq = wq.reshape(MLA_Q_LORA, MLA_HEADS * LANES).astype(_MX)
    pq = jnp.stack([
        jnp.tile(jnp.concatenate([ones(64) / MLA_NOPE, ones(64) / MLA_ROPE]), MLA_HEADS),
        jnp.tile(jnp.concatenate([q_g, zeros(32)]), MLA_HEADS),
    ])
    ukv = w_ukv.reshape(MLA_KV_LORA, MLA_HEADS, MLA_NOPE + MLA_V)
    wk = jnp.pad(ukv[:, :, :MLA_NOPE], ((0, 0), (0, 0), (0, 64))).reshape(MLA_KV_LORA, MLA_HEADS * LANES).astype(_MX)
    wv = ukv[:, :, MLA_NOPE:].reshape(MLA_KV_LORA, MLA_HEADS * MLA_V).astype(_MX)
    pk = jnp.stack([
        jnp.tile(ones(LANES) / MLA_NOPE, MLA_HEADS),
        jnp.tile(jnp.concatenate([k_g[:MLA_NOPE], zeros(64)]), MLA_HEADS),
    ])
    return win, wq, wk, wv, pin, pq, pk, w_out.astype(_MX)


def _odd_weights(w_in, q_g, k_g, w_out):
    o1 = SPA_HEADS * SPA_HD
    o2, o3 = o1 + SPA_HD, o1 + 2 * SPA_HD
    o4 = o3 + IDX_HEADS * IDX_HD
    o5 = o4 + IDX_HD
    d = w_in.shape[0]
    k, v, ki = w_in[:, o1:o2], w_in[:, o2:o3], w_in[:, o4:o5]
    win = jnp.concatenate([w_in[:, :o1], k, k, v, v, w_in[:, o3:o4], ki, ki, w_in[:, o5:],
                           jnp.zeros((d, LANES - IDX_HEADS), _F32)], axis=1).astype(_MX)
    ones = jnp.ones((ODD_IN_PAD,), _F32)
    gain = jnp.concatenate([jnp.tile(q_g, SPA_HEADS), k_g, k_g, jnp.ones((ODD_IN_PAD - o1 - 2 * SPA_HD,), _F32)])
    pin = jnp.stack([ones / SPA_HD, gain])
    return win, pin, w_out.astype(_MX)


def kernel(x, mem, positions, ffn1_norm, ffn1_w13, ffn1_w2, mix_norm, xattn_norm, mem_norm,
           xattn_wq, xattn_wkv, xattn_q_gain, xattn_k_gain, xattn_wo, ffn2_norm, ffn2_w13, ffn2_w2,
           even_w_in, mla_q_lora_norm, mla_kv_lora_norm, mla_w_uq, mla_w_ukv, mla_q_gain, mla_k_gain,
           dil_q_gain, dil_k_gain, even_w_out, odd_w_in, sparse_q_gain, sparse_k_gain, odd_w_out):
    batch, seq, d = x.shape
    mem_len = mem.shape[1]
    n = batch * seq
    c64, n64, cm, nm = _tables(positions)
    s64, s128, s256 = _seg_matrix(64), _seg_matrix(128), _seg_matrix(256)
    bf = lambda a: a.astype(_MX)
    row3 = lambda a: a.reshape(a.shape[0], 1, a.shape[1])
    f1w13, f1w2, f2w13, f2w2 = bf(ffn1_w13), bf(ffn1_w2), bf(ffn2_w13), bf(ffn2_w2)
    f1g, f2g = row3(ffn1_norm), row3(ffn2_norm)
    xwq, xwkv, xwo = bf(xattn_wq), bf(xattn_wkv), bf(xattn_wo)

    xs = x.reshape(n, d)
    mems = mem.reshape(batch * mem_len, d)
    for i in range(DEPTH):
        j = i // 2
        xs = _ffn(xs, f1g, f1w13, f1w2, i)
        g_mix = mix_norm[i][None, :]
        if i % 2 == 0:
            win, wq, wk, wv, pin, pq, pk, wout = _even_weights(
                even_w_in[j], mla_q_lora_norm[j], mla_kv_lora_norm[j], mla_w_uq[j], mla_w_ukv[j],
                mla_q_gain[j], mla_k_gain[j], dil_q_gain[j], dil_k_gain[j], even_w_out[j])
            qm, km, vt, dil = _even_proj(xs, g_mix, win, wq, wk, wv, pin, pq, pk,
                                         (s64, s128, s256), (c64, n64, cm, nm))
            n_mla = MLA_HEADS * MLA_V
            mix = [_mla_attn(qm, km, vt, batch, seq), _dil_attn(dil, batch, seq)]
            mix_w = [wout[:n_mla], wout[n_mla:]]
        else:
            win, pin, wout = _odd_weights(odd_w_in[j], sparse_q_gain[j], sparse_k_gain[j], odd_w_out[j])
            q, kk, vt, qi, ki, wit = _odd_proj(xs, g_mix, win, pin, s64, (c64, n64))
            mix = [_sparse_attn(q, kk, vt, qi, ki, wit, batch, seq)]
            mix_w = [wout]
        mk, mv = _mem_kv(mems, mem_norm[i][None, :], xwkv[i], xattn_k_gain[i][None, :])
        xs = _xattn(xs, mix, mix_w, xattn_norm[i][None, :], xwq[i], xattn_q_gain[i][None, :], mk, mv, xwo[i],
                    batch, seq, mem_len)
        xs = _ffn(xs, f2g, f2w13, f2w2, i)
    return xs.reshape(batch, seq, d)
```
